```python
import math
import jax
import jax.numpy as jnp
from jax import lax
import numpy as np


D_MODEL = 1024
BATCH = 8
SEQ = 4096
DEPTH = 4

HEAD_DIM = 64
N_Q_HEADS = 8
N_KV_HEADS = 2
Q_PER_KV = N_Q_HEADS // N_KV_HEADS
ATTN_WIDTH = N_Q_HEADS * HEAD_DIM
KV_WIDTH = N_KV_HEADS * HEAD_DIM
N_BRANCH = 3
SSM_WIDTH = D_MODEL - ATTN_WIDTH
SSM_GROUP = 16
N_SSM_GROUPS = SSM_WIDTH // SSM_GROUP
SSM_STATE = 64
MIX_WIDTH = ATTN_WIDTH + SSM_WIDTH
PROJ_WIDTH = ATTN_WIDTH + 6 * KV_WIDTH + N_BRANCH * N_Q_HEADS + SSM_WIDTH

CMP_LEN = 32
CMP_STRIDE = 16
SLC_LEN = 64
N_SEL = 16
WINDOW = 512
Q_BLOCK = 128

DT_MIN = 1e-3
DT_MAX = 1e-1

D_FF = 2816
N_EXPERTS = 8
TOP_K = 2
N_DENSE = (DEPTH + 1) // 2
N_MOE = DEPTH // 2

DEEPNORM_ALPHA = (2.0 * DEPTH) ** 0.25
DEEPNORM_BETA = (8.0 * DEPTH) ** -0.25
LN_EPS = 1e-5
RMS_EPS = 1e-6

kernel_name = 'hybrid_nsa_s5_moe_deepnorm_block'


def layer_norm(x, g, b):
    x32 = x.astype(jnp.float32)
    mu = jnp.mean(x32, axis=-1, keepdims=True)
    var = jnp.mean(jnp.square(x32 - mu), axis=-1, keepdims=True)
    y = (x32 - mu) * lax.rsqrt(var + LN_EPS)
    return (y * g.astype(jnp.float32) + b.astype(jnp.float32)).astype(x.dtype)


def rms_norm(x, g):
    x32 = x.astype(jnp.float32)
    y = x32 * lax.rsqrt(jnp.mean(jnp.square(x32), axis=-1, keepdims=True) + RMS_EPS)
    return (y * g.astype(jnp.float32)).astype(x.dtype)


def masked_softmax(s, mask):
    s = jnp.where(mask, s.astype(jnp.float32), -jnp.inf)
    m = jnp.max(s, axis=-1, keepdims=True)
    m = jnp.where(jnp.isfinite(m), m, 0.0)
    e = jnp.exp(s - m)
    den = jnp.sum(e, axis=-1, keepdims=True)
    return e / jnp.where(den > 0, den, 1.0)


def swiglu(h, w_gate, w_up, w_down):
    return (jax.nn.silu(h @ w_gate) * (h @ w_up)) @ w_down


def moe_swiglu(h, router, w_gate, w_up, w_down):
    logits = (h @ router).astype(jnp.float32)
    top_val, top_idx = lax.top_k(logits, TOP_K)
    top_w = jax.nn.softmax(top_val, axis=-1)
    gate = jnp.sum(jax.nn.one_hot(top_idx, N_EXPERTS, dtype=jnp.float32) * top_w[..., None],
                   axis=-2).astype(h.dtype)
    out = jnp.zeros_like(h)
    for e in range(N_EXPERTS):
        out = out + gate[..., e:e + 1] * swiglu(h, w_gate[e], w_up[e], w_down[e])
    return out


def compress_blocks(k, pos, w1, w2):
    L = k.shape[1]
    n_cmp = (L - CMP_LEN) // CMP_STRIDE + 1
    idx = np.arange(n_cmp)[:, None] * CMP_STRIDE + np.arange(CMP_LEN)[None, :]
    blocks = k[:, idx] + pos[None, None, :, None, :]
    hid = jax.nn.gelu(jnp.einsum('bnlhd,ldf->bnhf', blocks, w1))
    return jnp.einsum('bnhf,fd->bnhd', hid, w2)


def nsa_attention(q, k_cmp, v_cmp, k_slc, v_slc, k_win, v_win, gates):
    B, L = q.shape[0], q.shape[1]
    scale = HEAD_DIM ** -0.5
    n_cmp = k_cmp.shape[1]
    n_slc = L // SLC_LEN
    n_sel = min(N_SEL, n_slc)
    cmp_end = jnp.arange(n_cmp) * CMP_STRIDE + CMP_LEN - 1
    cs = np.arange(n_cmp)[:, None] * CMP_STRIDE
    ss = np.arange(n_slc)[None, :] * SLC_LEN
    overlap = jnp.asarray((cs < ss + SLC_LEN) & (cs + CMP_LEN > ss), jnp.float32)
    kb = k_slc.reshape(B, n_slc, SLC_LEN, N_KV_HEADS, HEAD_DIM).transpose(0, 3, 1, 2, 4)
    vb = v_slc.reshape(B, n_slc, SLC_LEN, N_KV_HEADS, HEAD_DIM).transpose(0, 3, 1, 2, 4)
    kw = jnp.pad(k_win, ((0, 0), (WINDOW, 0), (0, 0), (0, 0)))
    vw = jnp.pad(v_win, ((0, 0), (WINDOW, 0), (0, 0), (0, 0)))
    gather = jax.vmap(jax.vmap(lambda blocks, idx: blocks[idx]))
    blk = jnp.arange(n_slc)

    def block(i):
        qs = i * Q_BLOCK
        qb = lax.dynamic_slice_in_dim(q, qs, Q_BLOCK, axis=1) * scale
        gb = lax.dynamic_slice_in_dim(gates, qs, Q_BLOCK, axis=1)
        t = qs + jnp.arange(Q_BLOCK)

        s = jnp.einsum('bqhgd,bnhd->bhgqn', qb, k_cmp)
        p_cmp = masked_softmax(s, cmp_end[None, :] <= t[:, None])
        o_cmp = jnp.einsum('bhgqn,bnhd->bqhgd', p_cmp.astype(v_cmp.dtype), v_cmp)

        imp = jnp.einsum('bhgqn,ns->bhqs', p_cmp, overlap)
        cur = t // SLC_LEN
        forced = (blk[None, :] == 0) | (blk[None, :] == cur[:, None]) | (blk[None, :] == cur[:, None] - 1)
        causal = blk[None, :] * SLC_LEN <= t[:, None]
        score = jnp.where(causal, jnp.where(forced, jnp.inf, imp), -jnp.inf)
        top_val, top_idx = lax.top_k(score, n_sel)
        sel_valid = top_val > -jnp.inf
        k_sel = gather(kb, top_idx)
        v_sel = gather(vb, top_idx)
        kpos = top_idx[..., None] * SLC_LEN + jnp.arange(SLC_LEN)
        m_sel = sel_valid[..., None] & (kpos <= t[None, None, :, None, None])
        s = jnp.einsum('bqhgd,bhqjld->bhgqjl', qb, k_sel).reshape(B, N_KV_HEADS, Q_PER_KV, Q_BLOCK, n_sel * SLC_LEN)
        p = masked_softmax(s, m_sel.reshape(B, N_KV_HEADS, Q_BLOCK, n_sel * SLC_LEN)[:, :, None])
        o_slc = jnp.einsum('bhgqk,bhqkd->bqhgd', p.astype(v_sel.dtype),
                           v_sel.reshape(B, N_KV_HEADS, Q_BLOCK, n_sel * SLC_LEN, HEAD_DIM))

        kwb = lax.dynamic_slice_in_dim(kw, qs, WINDOW + Q_BLOCK, axis=1)
        vwb = lax.dynamic_slice_in_dim(vw, qs, WINDOW + Q_BLOCK, axis=1)
        wpos = qs - WINDOW + jnp.arange(WINDOW + Q_BLOCK)
        dist = t[:, None] - wpos[None, :]
        m_win = (dist >= 0) & (dist < WINDOW) & (wpos[None, :] >= 0)
        s = jnp.einsum('bqhgd,bkhd->bhgqk', qb, kwb)
        p = masked_softmax(s, m_win)
        o_win = jnp.einsum('bhgqk,bkhd->bqhgd', p.astype(vwb.dtype), vwb)

        g = jax.nn.sigmoid(gb.astype(jnp.float32))
        o = (g[..., 0:1] * o_cmp.astype(jnp.float32) + g[..., 1:2] * o_slc.astype(jnp.float32)
             + g[..., 2:3] * o_win.astype(jnp.float32))
        return o.astype(q.dtype)

    out = lax.map(block, jnp.arange(L // Q_BLOCK))
    return out.transpose(1, 0, 2, 3, 4, 5).reshape(B, L, ATTN_WIDTH)


def cplx(re, im):
    return lax.complex(re.astype(jnp.float32), im.astype(jnp.float32))


def s5_ssm(u, a_re, a_im, log_dt, b_re, b_im, c_re, c_im, d_skip):
    B, L = u.shape[0], u.shape[1]
    u32 = u.astype(jnp.float32).reshape(B, L, N_SSM_GROUPS, SSM_GROUP)
    a = cplx(a_re, a_im)
    dt = jnp.exp(log_dt.astype(jnp.float32))[:, None]
    a_bar = jnp.exp(a * dt)
    b_bar = ((a_bar - 1.0) / a)[..., None] * cplx(b_re, b_im)
    bu = jnp.einsum('blgh,gph->blgp', u32.astype(jnp.complex64), b_bar)
    a_seq = jnp.broadcast_to(a_bar, (1, L) + a_bar.shape)

    def combine(left, right):
        return (right[0] * left[0], right[0] * left[1] + right[1])

    _, states = lax.associative_scan(combine, (a_seq, bu), axis=1)
    y = jnp.einsum('blgp,ghp->blgh', states, cplx(c_re, c_im)).real + d_skip.astype(jnp.float32) * u32
    return y.reshape(B, L, SSM_WIDTH)


def hybrid_mixer(h, w_in, cmp_pos_k, cmp_pos_v, cmp_w1_k, cmp_w2_k, cmp_w1_v, cmp_w2_v,
                 a_re, a_im, log_dt, b_re, b_im, c_re, c_im, d_skip, w_glu, norm_attn, norm_ssm, w_out):
    B, L = h.shape[0], h.shape[1]
    sizes = [ATTN_WIDTH] + [KV_WIDTH] * 6 + [N_BRANCH * N_Q_HEADS, SSM_WIDTH]
    cuts = np.cumsum(sizes)[:-1].tolist()
    q, kc, vc, ks, vs, kw, vw, g, u = jnp.split(h @ w_in, cuts, axis=-1)

    def kv(t):
        return t.reshape(B, L, N_KV_HEADS, HEAD_DIM)

    k_cmp = compress_blocks(kv(kc), cmp_pos_k, cmp_w1_k, cmp_w2_k)
    v_cmp = compress_blocks(kv(vc), cmp_pos_v, cmp_w1_v, cmp_w2_v)
    o_attn = nsa_attention(q.reshape(B, L, N_KV_HEADS, Q_PER_KV, HEAD_DIM), k_cmp, v_cmp,
                           kv(ks), kv(vs), kv(kw), kv(vw),
                           g.reshape(B, L, N_KV_HEADS, Q_PER_KV, N_BRANCH))
    y = jax.nn.gelu(s5_ssm(u, a_re, a_im, log_dt, b_re, b_im, c_re, c_im, d_skip).astype(h.dtype))
    o_ssm = y * jax.nn.sigmoid(y @ w_glu)
    o = jnp.concatenate([rms_norm(o_attn, norm_attn), rms_norm(o_ssm, norm_ssm)], axis=-1)
    return o @ w_out


def setup_inputs(seed: int = 0) -> dict:
    key = jax.random.key(seed)
    keys = jax.random.split(key, 32)

    def nrm(i, shape, scale):
        return scale * jax.random.normal(keys[i], shape, jnp.float32)

    G, P, H = N_SSM_GROUPS, SSM_STATE, SSM_GROUP
    n_idx = jnp.arange(P, dtype=jnp.float32)
    return {
        'x': nrm(0, (BATCH, SEQ, D_MODEL), 1.0),
        'c': nrm(1, (BATCH, D_MODEL), 1.0),
        'w_in': nrm(2, (DEPTH, D_MODEL, PROJ_WIDTH), D_MODEL ** -0.5),
        'cmp_pos_k': nrm(3, (DEPTH, CMP_LEN, HEAD_DIM), 0.1),
        'cmp_pos_v': nrm(4, (DEPTH, CMP_LEN, HEAD_DIM), 0.1),
        'cmp_w1_k': nrm(5, (DEPTH, CMP_LEN, HEAD_DIM, HEAD_DIM), (CMP_LEN * HEAD_DIM) ** -0.5),
        'cmp_w2_k': nrm(6, (DEPTH, HEAD_DIM, HEAD_DIM), HEAD_DIM ** -0.5),
        'cmp_w1_v': nrm(7, (DEPTH, CMP_LEN, HEAD_DIM, HEAD_DIM), (CMP_LEN * HEAD_DIM) ** -0.5),
        'cmp_w2_v': nrm(8, (DEPTH, HEAD_DIM, HEAD_DIM), HEAD_DIM ** -0.5),
        'ssm_a_re': -0.5 + nrm(9, (DEPTH, G, P), 0.01),
        'ssm_a_im': math.pi * n_idx + nrm(10, (DEPTH, G, P), 0.01),
        'ssm_log_dt': jax.random.uniform(keys[11], (DEPTH, G), jnp.float32, math.log(DT_MIN), math.log(DT_MAX)),
        'ssm_b_re': nrm(12, (DEPTH, G, P, H), (2.0 * H) ** -0.5),
        'ssm_b_im': nrm(13, (DEPTH, G, P, H), (2.0 * H) ** -0.5),
        'ssm_c_re': nrm(14, (DEPTH, G, H, P), (2.0 * P) ** -0.5),
        'ssm_c_im': nrm(15, (DEPTH, G, H, P), (2.0 * P) ** -0.5),
        'ssm_d': nrm(16, (DEPTH, G, H), 1.0),
        'ssm_w_glu': nrm(17, (DEPTH, SSM_WIDTH, SSM_WIDTH), SSM_WIDTH ** -0.5),
        'norm_attn': 1.0 + nrm(18, (DEPTH, ATTN_WIDTH), 0.01),
        'norm_ssm': 1.0 + nrm(19, (DEPTH, SSM_WIDTH), 0.01),
        'w_out': nrm(20, (DEPTH, MIX_WIDTH, D_MODEL), DEEPNORM_BETA * MIX_WIDTH ** -0.5),
        'ada_w': nrm(21, (DEPTH, D_MODEL, 6 * D_MODEL), 0.1 * D_MODEL ** -0.5),
        'ada_b': nrm(22, (DEPTH, 6 * D_MODEL), 0.01),
        'ln_g': 1.0 + nrm(23, (DEPTH, 2, D_MODEL), 0.01),
        'ln_b': nrm(24, (DEPTH, 2, D_MODEL), 0.01),
        'ffn_w_gate': nrm(25, (N_DENSE, D_MODEL, D_FF), D_MODEL ** -0.5),
        'ffn_w_up': nrm(26, (N_DENSE, D_MODEL, D_FF), D_MODEL ** -0.5),
        'ffn_w_down': nrm(27, (N_DENSE, D_FF, D_MODEL), DEEPNORM_BETA * D_FF ** -0.5),
        'moe_router': nrm(28, (N_MOE, D_MODEL, N_EXPERTS), D_MODEL ** -0.5),
        'moe_w_gate': nrm(29, (N_MOE, N_EXPERTS, D_MODEL, D_FF), D_MODEL ** -0.5),
        'moe_w_up': nrm(30, (N_MOE, N_EXPERTS, D_MODEL, D_FF), D_MODEL ** -0.5),
        'moe_w_down': nrm(31, (N_MOE, N_EXPERTS, D_FF, D_MODEL), DEEPNORM_BETA * D_FF ** -0.5),
    }


def reference(x, c, w_in, cmp_pos_k, cmp_pos_v, cmp_w1_k, cmp_w2_k, cmp_w1_v, cmp_w2_v,
              ssm_a_re, ssm_a_im, ssm_log_dt, ssm_b_re, ssm_b_im, ssm_c_re, ssm_c_im, ssm_d,
              ssm_w_glu, norm_attn, norm_ssm, w_out, ada_w, ada_b, ln_g, ln_b,
              ffn_w_gate, ffn_w_up, ffn_w_down, moe_router, moe_w_gate, moe_w_up, moe_w_down):
    c_act = jax.nn.silu(c)
    for l in range(DEPTH):
        mod = (c_act @ ada_w[l] + ada_b[l])[:, None, :]
        sh1, sc1, g1, sh2, sc2, g2 = jnp.split(mod, 6, axis=-1)
        y = hybrid_mixer(x * (1.0 + sc1) + sh1, w_in[l], cmp_pos_k[l], cmp_pos_v[l],
                         cmp_w1_k[l], cmp_w2_k[l], cmp_w1_v[l], cmp_w2_v[l],
                         ssm_a_re[l], ssm_a_im[l], ssm_log_dt[l], ssm_b_re[l], ssm_b_im[l],
                         ssm_c_re[l], ssm_c_im[l], ssm_d[l], ssm_w_glu[l],
                         norm_attn[l], norm_ssm[l], w_out[l])
        x = layer_norm(DEEPNORM_ALPHA * x + (1.0 + g1) * y, ln_g[l, 0], ln_b[l, 0])
        h = x * (1.0 + sc2) + sh2
        if l % 2 == 0:
            f = swiglu(h, ffn_w_gate[l // 2], ffn_w_up[l // 2], ffn_w_down[l // 2])
        else:
            f = moe_swiglu(h, moe_router[l // 2], moe_w_gate[l // 2], moe_w_up[l // 2], moe_w_down[l // 2])
        x = layer_norm(DEEPNORM_ALPHA * x + (1.0 + g2) * f, ln_g[l, 1], ln_b[l, 1])
    return x
```

```python
import functools
import math

import jax
import jax.numpy as jnp
from jax import lax
from jax.experimental import pallas as pl
from jax.experimental.pallas import tpu as pltpu

F32 = jnp.float32
BF16 = jnp.bfloat16

D_MODEL = 1024
DEPTH = 4
HEAD_DIM = 64
N_Q_HEADS = 8
N_KV_HEADS = 2
Q_PER_KV = 4
ATTN_WIDTH = 512
KV_WIDTH = 128
SSM_WIDTH = 512
SSM_GROUP = 16
N_SSM_GROUPS = 32
SSM_STATE = 64
CMP_LEN = 32
CMP_STRIDE = 16
SLC_LEN = 64
N_SEL = 16
WINDOW = 512
Q_BLOCK = 128
D_FF = 2816
N_EXPERTS = 8
DEEPNORM_ALPHA = (2.0 * DEPTH) ** 0.25
LN_EPS = 1e-5
RMS_EPS = 1e-6

PROJ_PAD = 1920
SSM_CHUNK = 16
SLC_KEY_CHUNK = 256
NEG = -1e30
VMEM_LIMIT = 56 * 1024 * 1024


def _cparams(sem):
    return pltpu.CompilerParams(dimension_semantics=sem, vmem_limit_bytes=VMEM_LIMIT)


def _sigmoid(x):
    return 1.0 / (1.0 + jnp.exp(-x))


def _gelu(x):
    return 0.5 * x * (1.0 + jnp.tanh(math.sqrt(2.0 / math.pi) * (x + 0.044715 * (x * x * x))))


def _layer_norm(x, g, b):
    mu = jnp.mean(x, axis=-1, keepdims=True)
    xc = x - mu
    var = jnp.mean(xc * xc, axis=-1, keepdims=True)
    return xc * lax.rsqrt(var + LN_EPS) * g + b


def _rms_norm(x, g):
    return x * lax.rsqrt(jnp.mean(x * x, axis=-1, keepdims=True) + RMS_EPS) * g


def _dot(a, b):
    return jnp.dot(a, b, preferred_element_type=F32)


def _dot_nt(a, b, precision=None):
    return lax.dot_general(a, b, (((1,), (1,)), ((), ())), preferred_element_type=F32,
                           precision=precision)


def _ada_kernel(c_ref, w_ref, b_ref, o_ref):
    c = c_ref[...]
    ca = (c * _sigmoid(c)).astype(BF16)
    o_ref[0] = _dot(ca, w_ref[0].astype(BF16)) + b_ref[0]


def _ada_mod(c, ada_w, ada_b):
    depth, d, n = ada_w.shape
    b = c.shape[0]
    tn = 1536
    return pl.pallas_call(
        _ada_kernel,
        grid=(depth, n // tn),
        in_specs=[pl.BlockSpec((b, d), lambda l, j: (0, 0)),
                  pl.BlockSpec((1, d, tn), lambda l, j: (l, 0, j)),
                  pl.BlockSpec((1, 1, tn), lambda l, j: (l, 0, j))],
        out_specs=pl.BlockSpec((1, b, tn), lambda l, j: (l, 0, j)),
        out_shape=jax.ShapeDtypeStruct((depth, b, n), F32),
        compiler_params=_cparams(("parallel", "parallel")),
        name="ada_mod",
    )(c, ada_w, ada_b.reshape(depth, 1, n))


def _proj_kernel(x_ref, mod_ref, w_ref, q_ref, u_ref, kc_ref, vc_ref, ks_ref, vs_ref, kw_ref,
                 vw_ref, g_ref):
    m = mod_ref[0]
    h = x_ref[0] * (1.0 + m[1:2]) + m[0:1]
    r = _dot(h.astype(BF16), w_ref[...])
    q_ref[0] = (r[:, 0:512] * (HEAD_DIM ** -0.5)).astype(BF16)
    u_ref[0] = r[:, 512:1024]
    kc_ref[0] = r[:, 1024:1152]
    vc_ref[0] = r[:, 1152:1280]
    ks_ref[0] = r[:, 1280:1408].astype(BF16)
    vs_ref[0] = r[:, 1408:1536].astype(BF16)
    kw_ref[0] = r[:, 1536:1664].astype(BF16)
    vw_ref[0] = r[:, 1664:1792].astype(BF16)
    g_ref[0] = r[:, 1792:1920]


def _in_proj(x, mod, w_pad):
    b, l, d = x.shape
    tm = 512
    row = lambda n: pl.BlockSpec((1, tm, n), lambda i, j: (i, j, 0))
    shp = lambda n, dt: jax.ShapeDtypeStruct((b, l, n), dt)
    return pl.pallas_call(
        _proj_kernel,
        grid=(b, l // tm),
        in_specs=[row(d),
                  pl.BlockSpec((1, 6, d), lambda i, j: (i, 0, 0)),
                  pl.BlockSpec((d, PROJ_PAD), lambda i, j: (0, 0))],
        out_specs=[row(512), row(512), row(128), row(128), row(128), row(128), row(128), row(128),
                   row(128)],
        out_shape=[shp(512, BF16), shp(512, F32), shp(128, F32), shp(128, F32), shp(128, BF16),
                   shp(128, BF16), shp(128, BF16), shp(128, BF16), shp(128, F32)],
        compiler_params=_cparams(("parallel", "parallel")),
        name="in_proj",
    )(x, mod, w_pad)


def _compress_one(kr, pos_a, pos_b, w1a, w1b, w2):
    nch = kr.shape[0]
    pa = _dot((kr + pos_a).astype(BF16), w1a)
    pb = _dot((kr + pos_b).astype(BF16), w1b)
    hid = _gelu(pa + pltpu.roll(pb, nch - 1, 0))
    out = _dot(hid.astype(BF16), w2)
    rows = lax.broadcasted_iota(jnp.int32, out.shape, 0)
    return jnp.where(rows < nch - 1, out, 0.0)


def _compress_kernel(k_ref, v_ref, pak_ref, pbk_ref, w1ak_ref, w1bk_ref, w2k_ref,
                     pav_ref, pbv_ref, w1av_ref, w1bv_ref, w2v_ref, ko_ref, vo_ref):
    ko_ref[0] = _compress_one(k_ref[0], pak_ref[...], pbk_ref[...], w1ak_ref[...], w1bk_ref[...],
                              w2k_ref[...]).astype(BF16)
    vo_ref[0] = _compress_one(v_ref[0], pav_ref[...], pbv_ref[...], w1av_ref[...], w1bv_ref[...],
                              w2v_ref[...]).astype(BF16)


def _compress_weights(pos, w1, w2):
    eye = jnp.eye(N_KV_HEADS, dtype=F32)
    half = CMP_STRIDE

    def big(w):
        return jnp.einsum('sdf,hg->shdgf', w, eye).reshape(half * KV_WIDTH, KV_WIDTH).astype(BF16)

    def posrow(p):
        return jnp.broadcast_to(p[:, None, :], (half, N_KV_HEADS, HEAD_DIM)).reshape(1, half * KV_WIDTH)

    w2b = jnp.einsum('fd,hg->hfgd', w2, eye).reshape(KV_WIDTH, KV_WIDTH).astype(BF16)
    return posrow(pos[:half]), posrow(pos[half:]), big(w1[:half]), big(w1[half:]), w2b


def _compress(kc, vc, wk, wv):
    b, l, _ = kc.shape
    nch = l // CMP_STRIDE
    kr = kc.reshape(b, nch, CMP_STRIDE * KV_WIDTH)
    vr = vc.reshape(b, nch, CMP_STRIDE * KV_WIDTH)
    full = lambda a: pl.BlockSpec(a.shape, lambda i: (0,) * a.ndim)
    blk = pl.BlockSpec((1, nch, CMP_STRIDE * KV_WIDTH), lambda i: (i, 0, 0))
    oblk = pl.BlockSpec((1, nch, KV_WIDTH), lambda i: (i, 0, 0))
    return pl.pallas_call(
        _compress_kernel,
        grid=(b,),
        in_specs=[blk, blk] + [full(a) for a in wk] + [full(a) for a in wv],
        out_specs=[oblk, oblk],
        out_shape=[jax.ShapeDtypeStruct((b, nch, KV_WIDTH), BF16)] * 2,
        compiler_params=_cparams(("parallel",)),
        name="compress",
    )(kr, vr, *wk, *wv)


def _attn_kernel(q_ref, g_ref, kc_ref, vc_ref, ks_ref, vs_ref, kw_ref, vw_ref, o_ref, *, seq):
    nch = seq // CMP_STRIDE
    n_cmp = nch - 1
    n_slc = seq // SLC_LEN
    n_sel = min(N_SEL, n_slc)
    qb, gq, hd = Q_BLOCK, Q_PER_KV, HEAD_DIM
    i = pl.program_id(1)
    qs = i * qb
    t_col = qs + lax.broadcasted_iota(jnp.int32, (qb, 1), 0)

    gates = _sigmoid(g_ref[0])
    n_idx = lax.broadcasted_iota(jnp.int32, (1, nch), 1)
    cmp_valid = ((n_idx * CMP_STRIDE + (CMP_LEN - 1)) <= t_col) & (n_idx < n_cmp)
    cmp_bias = jnp.where(cmp_valid, 0.0, NEG)
    cmp_keep = cmp_valid.astype(F32)
    on = lax.broadcasted_iota(jnp.int32, (nch, n_slc), 0) * CMP_STRIDE
    oj = lax.broadcasted_iota(jnp.int32, (nch, n_slc), 1) * SLC_LEN
    overlap = ((on < oj + SLC_LEN) & (on + CMP_LEN > oj)).astype(F32)

    j_idx = lax.broadcasted_iota(jnp.int32, (1, n_slc), 1)
    cur = t_col >> 6
    causal = (j_idx * SLC_LEN) <= t_col
    forced = (j_idx == 0) | (j_idx == cur) | (j_idx == cur - 1)

    win_start = jnp.maximum(qs - WINDOW, 0)
    wpos = win_start + lax.broadcasted_iota(jnp.int32, (1, WINDOW + qb), 1)
    dist = t_col - wpos
    win_bias = jnp.where((dist >= 0) & (dist < WINDOW), 0.0, NEG)

    n_chunks = (qs + qb + SLC_KEY_CHUNK - 1) // SLC_KEY_CHUNK
    j_rows = lax.broadcasted_iota(jnp.int32, (n_slc, 1), 0)

    outs = []
    for h in range(N_KV_HEADS):
        lo = h * hd
        q4 = jnp.concatenate(
            [q_ref[0, :, (gq * h + g) * hd:(gq * h + g + 1) * hd] for g in range(gq)], axis=0)

        s = _dot_nt(q4, kc_ref[0, :, lo:lo + hd]).reshape(gq, qb, nch) + cmp_bias[None]
        m = jnp.max(s, axis=-1, keepdims=True)
        e = jnp.exp(s - m) * cmp_keep[None]
        den = jnp.sum(e, axis=-1, keepdims=True)
        p = e / jnp.where(den > 0, den, 1.0)
        o_cmp = _dot(p.reshape(gq * qb, nch).astype(BF16), vc_ref[0, :, lo:lo + hd])
        p_sum = p[0] + p[1] + p[2] + p[3]
        imp = jnp.dot(p_sum, overlap, preferred_element_type=F32,
                      precision=lax.Precision.HIGHEST)

        score = jnp.where(causal, jnp.where(forced, jnp.inf, imp), -jnp.inf)
        rank = jnp.zeros((qb, n_slc), jnp.int32)
        for b in range(n_slc):
            cb = score[:, b:b + 1]
            beats = (cb > score) | ((cb == score) & (j_idx > b))
            rank = rank + beats.astype(jnp.int32)
        sel = ((rank < n_sel) & causal).astype(BF16)

        def slc_step(c, carry):
            m_i, l_i, acc = carry
            k0 = pl.multiple_of(c * SLC_KEY_CHUNK, SLC_KEY_CHUNK)
            kblk = ks_ref[0, pl.ds(k0, SLC_KEY_CHUNK), lo:lo + hd]
            vblk = vs_ref[0, pl.ds(k0, SLC_KEY_CHUNK), lo:lo + hd]
            kpos = k0 + lax.broadcasted_iota(jnp.int32, (1, SLC_KEY_CHUNK), 1)
            expand = ((kpos >> 6) == j_rows).astype(BF16)
            picked = _dot(sel, expand)
            bias = jnp.where((picked > 0.5) & (kpos <= t_col), 0.0, NEG)
            sc = (_dot_nt(q4, kblk).reshape(gq, qb, SLC_KEY_CHUNK) + bias[None]).reshape(
                gq * qb, SLC_KEY_CHUNK)
            m_new = jnp.maximum(m_i, jnp.max(sc, axis=-1, keepdims=True))
            alpha = jnp.exp(m_i - m_new)
            pr = jnp.exp(sc - m_new)
            l_new = alpha * l_i + jnp.sum(pr, axis=-1, keepdims=True)
            acc_new = alpha * acc + _dot(pr.astype(BF16), vblk)
            return m_new, l_new, acc_new

        init = (jnp.full((gq * qb, 1), NEG, F32), jnp.zeros((gq * qb, 1), F32),
                jnp.zeros((gq * qb, hd), F32))
        _, l_f, acc_f = lax.fori_loop(0, n_chunks, slc_step, init)
        o_slc = acc_f / l_f

        kwb = kw_ref[0, pl.ds(pl.multiple_of(win_start, qb), WINDOW + qb), lo:lo + hd]
        vwb = vw_ref[0, pl.ds(pl.multiple_of(win_start, qb), WINDOW + qb), lo:lo + hd]
        sw = (_dot_nt(q4, kwb).reshape(gq, qb, WINDOW + qb) + win_bias[None]).reshape(
            gq * qb, WINDOW + qb)
        mw = jnp.max(sw, axis=-1, keepdims=True)
        ew = jnp.exp(sw - mw)
        o_win = _dot(ew.astype(BF16), vwb) / jnp.sum(ew, axis=-1, keepdims=True)

        for g in range(gq):
            c0 = (gq * h + g) * 3
            rows = slice(g * qb, (g + 1) * qb)
            outs.append(gates[:, c0:c0 + 1] * o_cmp[rows] + gates[:, c0 + 1:c0 + 2] * o_slc[rows]
                        + gates[:, c0 + 2:c0 + 3] * o_win[rows])
    o_ref[0] = jnp.concatenate(outs, axis=1)


def _attention(q, g, k_cmp, v_cmp, ks, vs, kw, vw):
    b, l, _ = q.shape
    nch = l // CMP_STRIDE
    tile = lambda n: pl.BlockSpec((1, Q_BLOCK, n), lambda i, j: (i, j, 0))
    whole = lambda r: pl.BlockSpec((1, r, KV_WIDTH), lambda i, j: (i, 0, 0))
    return pl.pallas_call(
        functools.partial(_attn_kernel, seq=l),
        grid=(b, l // Q_BLOCK),
        in_specs=[tile(ATTN_WIDTH), tile(128), whole(nch), whole(nch), whole(l), whole(l),
                  whole(l), whole(l)],
        out_specs=tile(ATTN_WIDTH),
        out_shape=jax.ShapeDtypeStruct((b, l, ATTN_WIDTH), F32),
        compiler_params=_cparams(("parallel", "arbitrary")),
        name="nsa_attention",
    )(q, g, k_cmp, v_cmp, ks, vs, kw, vw)


def _ssm_param_kernel(are_ref, aim_ref, ldt_ref, btr_ref, bti_ref, cr_ref, ci_ref,
                      t_ref, bcr_ref, bci_ref, ccr_ref, cci_ref, a16_ref):
    tc, hg, p = SSM_CHUNK, SSM_GROUP, SSM_STATE
    rows = tc * hg
    a_re = are_ref[0]
    a_im = aim_ref[0]
    dt = jnp.exp(ldt_ref[0])
    lam_re = a_re * dt
    lam_im = a_im * dt

    def apow(k):
        mag = jnp.exp(k * lam_re)
        return mag * jnp.cos(k * lam_im), mag * jnp.sin(k * lam_im)

    one = jnp.ones((1, 1), F32)
    ab_re, ab_im = apow(one)
    nr, ni = ab_re - 1.0, ab_im
    inv = 1.0 / (a_re * a_re + a_im * a_im)
    cf_re = (nr * a_re + ni * a_im) * inv
    cf_im = (ni * a_re - nr * a_im) * inv
    bt_re, bt_im = btr_ref[0], bti_ref[0]
    bb_re = cf_re * bt_re - cf_im * bt_im
    bb_im = cf_re * bt_im + cf_im * bt_re
    bb_re = jnp.concatenate([bb_re] * tc, axis=0)
    bb_im = jnp.concatenate([bb_im] * tc, axis=0)
    c_re = jnp.concatenate([cr_ref[0]] * tc, axis=0)
    c_im = jnp.concatenate([ci_ref[0]] * tc, axis=0)
    step = (lax.broadcasted_iota(jnp.int32, (rows, 1), 0) >> 4).astype(F32)

    def cmul(xr, xi, yr, yi):
        return xr * yr - xi * yi, xr * yi + xi * yr

    l_re, l_im = cmul(bb_re, bb_im, *apow(-step))
    r_re, r_im = cmul(c_re, c_im, *apow(step))
    hp = lax.Precision.HIGHEST
    tm = _dot_nt(l_re, r_re, hp) - _dot_nt(l_im, r_im, hp)
    s_row = lax.broadcasted_iota(jnp.int32, (rows, rows), 0) >> 4
    t_col = lax.broadcasted_iota(jnp.int32, (rows, rows), 1) >> 4
    t_ref[0] = jnp.where(t_col >= s_row, tm, 0.0).astype(BF16)
    e_re, e_im = cmul(bb_re, bb_im, *apow((tc - 1.0) - step))
    bcr_ref[0] = e_re.astype(BF16)
    bci_ref[0] = e_im.astype(BF16)
    f_re, f_im = cmul(c_re, c_im, *apow(step + 1.0))
    ccr_ref[0] = f_re.astype(BF16)
    cci_ref[0] = (-f_im).astype(BF16)
    p_re, p_im = apow(float(tc) * one)
    a16_ref[0] = jnp.concatenate([p_re, p_im], axis=0)


def _ssm_params(a_re, a_im, log_dt, b_re, b_im, c_re, c_im):
    g, p = a_re.shape
    hg, tc = SSM_GROUP, SSM_CHUNK
    rows = tc * hg
    vec = pl.BlockSpec((1, 1, p), lambda i: (i, 0, 0))
    mat = pl.BlockSpec((1, hg, p), lambda i: (i, 0, 0))
    out_rp = pl.BlockSpec((1, rows, p), lambda i: (i, 0, 0))
    return pl.pallas_call(
        _ssm_param_kernel,
        grid=(g,),
        in_specs=[vec, vec, pl.BlockSpec((1, 1, 1), lambda i: (i, 0, 0)), mat, mat, mat, mat],
        out_specs=[pl.BlockSpec((1, rows, rows), lambda i: (i, 0, 0)), out_rp, out_rp, out_rp, out_rp,
                   pl.BlockSpec((1, 2, p), lambda i: (i, 0, 0))],
        out_shape=[jax.ShapeDtypeStruct((g, rows, rows), BF16)] +
                  [jax.ShapeDtypeStruct((g, rows, p), BF16)] * 4 +
                  [jax.ShapeDtypeStruct((g, 2, p), F32)],
        compiler_params=_cparams(("parallel",)),
        name="ssm_params",
    )(a_re.reshape(g, 1, p), a_im.reshape(g, 1, p), log_dt.reshape(g, 1, 1),
      jnp.swapaxes(b_re, 1, 2), jnp.swapaxes(b_im, 1, 2), c_re, c_im)


def _ssm_kernel(u_ref, t_ref, bcr_ref, bci_ref, ccr_ref, cci_ref, a16_ref, d_ref, y_ref,
                wre, wim, xre, xim, *, batch, n_chunks):
    u = u_ref[0]
    ub = u.astype(BF16)
    wre[...] = _dot(ub, bcr_ref[0])
    wim[...] = _dot(ub, bci_ref[0])
    a16 = a16_ref[0]
    a_r = a16[0:1]
    a_i = a16[1:2]

    def step(c, carry):
        x_r, x_i = carry
        r0 = c * batch
        if batch % 8 == 0:
            r0 = pl.multiple_of(r0, 8)
        xre[pl.ds(r0, batch), :] = x_r
        xim[pl.ds(r0, batch), :] = x_i
        w_r = wre[pl.ds(r0, batch), :]
        w_i = wim[pl.ds(r0, batch), :]
        return a_r * x_r - a_i * x_i + w_r, a_r * x_i + a_i * x_r + w_i

    zero = jnp.zeros((batch, SSM_STATE), F32)
    lax.fori_loop(0, n_chunks, step, (zero, zero))
    y = _dot(ub, t_ref[0])
    y = y + _dot_nt(xre[...].astype(BF16), ccr_ref[0]) + _dot_nt(xim[...].astype(BF16), cci_ref[0])
    y_ref[0] = y + d_ref[0] * u


def _ssm(u, params, d_skip):
    b, l, _ = u.shape
    g, hg, tc, p = N_SSM_GROUPS, SSM_GROUP, SSM_CHUNK, SSM_STATE
    nch = l // tc
    rows = nch * b
    w = tc * hg
    ug = u.reshape(b, nch, tc, g, hg).transpose(3, 1, 0, 2, 4).reshape(g, rows, w)
    d_t = jnp.broadcast_to(d_skip[:, None, :], (g, tc, hg)).reshape(g, 1, w)
    t_op, bcr, bci, ccr, cci, a16 = params
    grp = lambda r, c: pl.BlockSpec((1, r, c), lambda i: (i, 0, 0))
    y = pl.pallas_call(
        functools.partial(_ssm_kernel, batch=b, n_chunks=nch),
        grid=(g,),
        in_specs=[grp(rows, w), grp(w, w), grp(w, p), grp(w, p), grp(w, p), grp(w, p), grp(2, p),
                  grp(1, w)],
        out_specs=grp(rows, w),
        out_shape=jax.ShapeDtypeStruct((g, rows, w), F32),
        scratch_shapes=[pltpu.VMEM((rows, p), F32)] * 4,
        compiler_params=_cparams(("parallel",)),
        name="ssm_scan",
    )(ug, t_op, bcr, bci, ccr, cci, a16, d_t)
    return y.reshape(g, nch, b, tc, hg).transpose(2, 1, 3, 0, 4).reshape(b, l, g * hg)


def _mix_out_kernel(oa_ref, y_ref, x_ref, mod_ref, wglu_ref, woa_ref, wos_ref, na_ref, ns_ref,
                    lg_ref, lb_ref, o_ref):
    m = mod_ref[0]
    ya = _gelu(y_ref[0])
    o_ssm = ya * _sigmoid(_dot(ya.astype(BF16), wglu_ref[...]))
    ra = _rms_norm(oa_ref[0], na_ref[...])
    rs = _rms_norm(o_ssm, ns_ref[...])
    mix = _dot(ra.astype(BF16), woa_ref[...]) + _dot(rs.astype(BF16), wos_ref[...])
    o_ref[0] = _layer_norm(DEEPNORM_ALPHA * x_ref[0] + (1.0 + m[2:3]) * mix, lg_ref[...], lb_ref[...])


def _mix_out(o_attn, y, x, mod, w_glu, w_out_a, w_out_s, norm_attn, norm_ssm, ln_g, ln_b):
    b, l, d = x.shape
    tm = 512
    row = lambda n: pl.BlockSpec((1, tm, n), lambda i, j: (i, j, 0))
    full = lambda a: pl.BlockSpec(a.shape, lambda i, j: (0,) * a.ndim)
    consts = (w_glu, w_out_a, w_out_s, norm_attn, norm_ssm, ln_g, ln_b)
    return pl.pallas_call(
        _mix_out_kernel,
        grid=(b, l // tm),
        in_specs=[row(ATTN_WIDTH), row(SSM_WIDTH), row(d),
                  pl.BlockSpec((1, 6, d), lambda i, j: (i, 0, 0))] + [full(a) for a in consts],
        out_specs=row(d),
        out_shape=jax.ShapeDtypeStruct((b, l, d), F32),
        compiler_params=_cparams(("parallel", "parallel")),
        name="mix_out",
    )(o_attn, y, x, mod, *consts)


FF_CHUNK = 1408


def _ffn_kernel(x_ref, mod_ref, wg_ref, wu_ref, wd_ref, lg_ref, lb_ref, o_ref, h_scr, acc):
    j = pl.program_id(2)
    m = mod_ref[0]

    @pl.when(j == 0)
    def _():
        h_scr[...] = (x_ref[0] * (1.0 + m[4:5]) + m[3:4]).astype(BF16)
        acc[...] = jnp.zeros_like(acc)

    h = h_scr[...]
    gate = _dot(h, wg_ref[...])
    up = _dot(h, wu_ref[...])
    act = (gate * _sigmoid(gate) * up).astype(BF16)
    acc[...] += _dot(act, wd_ref[...])

    @pl.when(j == pl.num_programs(2) - 1)
    def _():
        o_ref[0] = _layer_norm(DEEPNORM_ALPHA * x_ref[0] + (1.0 + m[5:6]) * acc[...],
                               lg_ref[...], lb_ref[...])


def _ffn(x, mod, wg, wu, wd, ln_g, ln_b):
    b, l, d = x.shape
    tm = 512
    nf = D_FF // FF_CHUNK
    return pl.pallas_call(
        _ffn_kernel,
        grid=(b, l // tm, nf),
        in_specs=[pl.BlockSpec((1, tm, d), lambda i, r, j: (i, r, 0)),
                  pl.BlockSpec((1, 6, d), lambda i, r, j: (i, 0, 0)),
                  pl.BlockSpec((d, FF_CHUNK), lambda i, r, j: (0, j)),
                  pl.BlockSpec((d, FF_CHUNK), lambda i, r, j: (0, j)),
                  pl.BlockSpec((FF_CHUNK, d), lambda i, r, j: (j, 0)),
                  pl.BlockSpec((1, d), lambda i, r, j: (0, 0)),
                  pl.BlockSpec((1, d), lambda i, r, j: (0, 0))],
        out_specs=pl.BlockSpec((1, tm, d), lambda i, r, j: (i, r, 0)),
        out_shape=jax.ShapeDtypeStruct((b, l, d), F32),
        scratch_shapes=[pltpu.VMEM((tm, d), BF16), pltpu.VMEM((tm, d), F32)],
        compiler_params=_cparams(("parallel", "parallel", "arbitrary")),
        name="ffn",
    )(x, mod, wg, wu, wd, ln_g, ln_b)


def _moe_kernel(x_ref, mod_ref, rt_ref, wg_ref, wu_ref, wd_ref, lg_ref, lb_ref, o_ref,
                h_scr, gate_scr, acc):
    e = pl.program_id(2)
    j = pl.program_id(3)
    m = mod_ref[0]
    lane = lax.broadcasted_iota(jnp.int32, gate_scr.shape, 1)

    @pl.when((e == 0) & (j == 0))
    def _():
        h = x_ref[0] * (1.0 + m[4:5]) + m[3:4]
        h_scr[...] = h.astype(BF16)
        acc[...] = jnp.zeros_like(acc)
        logits = jnp.dot(h, rt_ref[...], preferred_element_type=F32, precision=lax.Precision.HIGHEST)
        logits = jnp.where(lane < N_EXPERTS, logits, -jnp.inf)
        lane_f = lane.astype(F32)
        v1 = jnp.max(logits, axis=-1, keepdims=True)
        i1 = jnp.min(jnp.where(logits == v1, lane_f, 128.0), axis=-1, keepdims=True)
        rest = jnp.where(lane_f == i1, -jnp.inf, logits)
        v2 = jnp.max(rest, axis=-1, keepdims=True)
        i2 = jnp.min(jnp.where(rest == v2, lane_f, 128.0), axis=-1, keepdims=True)
        e2 = jnp.exp(v2 - v1)
        den = 1.0 + e2
        gate_scr[...] = (jnp.where(lane_f == i1, 1.0 / den, 0.0)
                         + jnp.where(lane_f == i2, e2 / den, 0.0))

    h = h_scr[...]
    gate = _dot(h, wg_ref[0])
    up = _dot(h, wu_ref[0])
    act = (gate * _sigmoid(gate) * up).astype(BF16)
    g_e = jnp.sum(jnp.where(lane == e, gate_scr[...], 0.0), axis=-1, keepdims=True)
    acc[...] += g_e * _dot(act, wd_ref[0])

    @pl.when((e == pl.num_programs(2) - 1) & (j == pl.num_programs(3) - 1))
    def _():
        o_ref[0] = _layer_norm(DEEPNORM_ALPHA * x_ref[0] + (1.0 + m[5:6]) * acc[...],
                               lg_ref[...], lb_ref[...])


def _moe(x, mod, router_pad, wg, wu, wd, ln_g, ln_b):
    b, l, d = x.shape
    tm = 512
    nf = D_FF // FF_CHUNK
    return pl.pallas_call(
        _moe_kernel,
        grid=(b, l // tm, N_EXPERTS, nf),
        in_specs=[pl.BlockSpec((1, tm, d), lambda i, r, e, j: (i, r, 0)),
                  pl.BlockSpec((1, 6, d), lambda i, r, e, j: (i, 0, 0)),
                  pl.BlockSpec((d, 128), lambda i, r, e, j: (0, 0)),
                  pl.BlockSpec((1, d, FF_CHUNK), lambda i, r, e, j: (e, 0, j)),
                  pl.BlockSpec((1, d, FF_CHUNK), lambda i, r, e, j: (e, 0, j)),
                  pl.BlockSpec((1, FF_CHUNK, d), lambda i, r, e, j: (e, j, 0)),
                  pl.BlockSpec((1, d), lambda i, r, e, j: (0, 0)),
                  pl.BlockSpec((1, d), lambda i, r, e, j: (0, 0))],
        out_specs=pl.BlockSpec((1, tm, d), lambda i, r, e, j: (i, r, 0)),
        out_shape=jax.ShapeDtypeStruct((b, l, d), F32),
        scratch_shapes=[pltpu.VMEM((tm, d), BF16), pltpu.VMEM((tm, 128), F32),
                        pltpu.VMEM((tm, d), F32)],
        compiler_params=_cparams(("parallel", "parallel", "arbitrary", "arbitrary")),
        name="moe",
    )(x, mod, router_pad, wg, wu, wd, ln_g, ln_b)


def _reorder_w_in(w):
    q = w[:, :ATTN_WIDTH]
    kv = w[:, ATTN_WIDTH:ATTN_WIDTH + 6 * KV_WIDTH]
    g0 = ATTN_WIDTH + 6 * KV_WIDTH
    gates = w[:, g0:g0 + 24]
    u = w[:, g0 + 24:]
    pad = jnp.zeros((w.shape[0], PROJ_PAD - w.shape[1]), w.dtype)
    return jnp.concatenate([q, u, kv, gates, pad], axis=1).astype(BF16)


def kernel(x, c, w_in, cmp_pos_k, cmp_pos_v, cmp_w1_k, cmp_w2_k, cmp_w1_v, cmp_w2_v, ssm_a_re,
           ssm_a_im, ssm_log_dt, ssm_b_re, ssm_b_im, ssm_c_re, ssm_c_im, ssm_d, ssm_w_glu, norm_attn,
           norm_ssm, w_out, ada_w, ada_b, ln_g, ln_b, ffn_w_gate, ffn_w_up, ffn_w_down, moe_router,
           moe_w_gate, moe_w_up, moe_w_down):
    b, l, d = x.shape
    mod_all = _ada_mod(c, ada_w, ada_b).reshape(DEPTH, b, 6, d)
    for layer in range(DEPTH):
        mod = mod_all[layer]
        q, u, kc, vc, ks, vs, kw, vw, g = _in_proj(x, mod, _reorder_w_in(w_in[layer]))
        k_cmp, v_cmp = _compress(
            kc, vc,
            _compress_weights(cmp_pos_k[layer], cmp_w1_k[layer], cmp_w2_k[layer]),
            _compress_weights(cmp_pos_v[layer], cmp_w1_v[layer], cmp_w2_v[layer]))
        o_attn = _attention(q, g, k_cmp, v_cmp, ks, vs, kw, vw)
        params = _ssm_params(ssm_a_re[layer], ssm_a_im[layer], ssm_log_dt[layer], ssm_b_re[layer],
                             ssm_b_im[layer], ssm_c_re[layer], ssm_c_im[layer])
        y = _ssm(u, params, ssm_d[layer])
        wo = w_out[layer].astype(BF16)
        x = _mix_out(o_attn, y, x, mod, ssm_w_glu[layer].astype(BF16), wo[:ATTN_WIDTH],
                     wo[ATTN_WIDTH:], norm_attn[layer][None], norm_ssm[layer][None],
                     ln_g[layer, 0][None], ln_b[layer, 0][None])
        lg, lb = ln_g[layer, 1][None], ln_b[layer, 1][None]
        if layer % 2 == 0:
            x = _ffn(x, mod, ffn_w_gate[layer // 2].astype(BF16), ffn_w_up[layer // 2].astype(BF16),
                     ffn_w_down[layer // 2].astype(BF16), lg, lb)
        else:
            rt = jnp.pad(moe_router[layer // 2], ((0, 0), (0, 128 - N_EXPERTS)))
            x = _moe(x, mod, rt, moe_w_gate[layer // 2].astype(BF16),
                     moe_w_up[layer // 2].astype(BF16), moe_w_down[layer // 2].astype(BF16), lg, lb)
    return x
```

```python
import functools
import math

import jax
import jax.numpy as jnp
from jax import lax
from jax.experimental import pallas as pl
from jax.experimental.pallas import tpu as pltpu

F32 = jnp.float32
BF16 = jnp.bfloat16

D_MODEL = 1024
DEPTH = 4
HEAD_DIM = 64
N_Q_HEADS = 8
N_KV_HEADS = 2
Q_PER_KV = 4
ATTN_WIDTH = 512
KV_WIDTH = 128
SSM_WIDTH = 512
SSM_GROUP = 16
N_SSM_GROUPS = 32
SSM_STATE = 64
CMP_LEN = 32
CMP_STRIDE = 16
SLC_LEN = 64
N_SEL = 16
WINDOW = 512
Q_BLOCK = 128
D_FF = 2816
N_EXPERTS = 8
DEEPNORM_ALPHA = (2.0 * DEPTH) ** 0.25
LN_EPS = 1e-5
RMS_EPS = 1e-6

PROJ_PAD = 1920
SSM_CHUNK = 16
SLC_KEY_CHUNK = 512
NEG = -1e30
VMEM_LIMIT = 56 * 1024 * 1024


def _cparams(sem):
    return pltpu.CompilerParams(dimension_semantics=sem, vmem_limit_bytes=VMEM_LIMIT)


def _sigmoid(x):
    return 1.0 / (1.0 + jnp.exp(-x))


def _gelu(x):
    return 0.5 * x * (1.0 + jnp.tanh(math.sqrt(2.0 / math.pi) * (x + 0.044715 * (x * x * x))))


def _layer_norm(x, g, b):
    mu = jnp.mean(x, axis=-1, keepdims=True)
    xc = x - mu
    var = jnp.mean(xc * xc, axis=-1, keepdims=True)
    return xc * lax.rsqrt(var + LN_EPS) * g + b


def _rms_norm(x, g):
    return x * lax.rsqrt(jnp.mean(x * x, axis=-1, keepdims=True) + RMS_EPS) * g


def _dot(a, b):
    return jnp.dot(a, b, preferred_element_type=F32)


def _dot_nt(a, b, precision=None):
    return lax.dot_general(a, b, (((1,), (1,)), ((), ())), preferred_element_type=F32,
                           precision=precision)


def _ada_kernel(c_ref, w_ref, b_ref, o_ref):
    c = c_ref[...]
    ca = (c * _sigmoid(c)).astype(BF16)
    o_ref[0] = _dot(ca, w_ref[0].astype(BF16)) + b_ref[0]


def _ada_mod(c, ada_w, ada_b):
    depth, d, n = ada_w.shape
    b = c.shape[0]
    tn = 1536
    return pl.pallas_call(
        _ada_kernel,
        grid=(depth, n // tn),
        in_specs=[pl.BlockSpec((b, d), lambda l, j: (0, 0)),
                  pl.BlockSpec((1, d, tn), lambda l, j: (l, 0, j)),
                  pl.BlockSpec((1, 1, tn), lambda l, j: (l, 0, j))],
        out_specs=pl.BlockSpec((1, b, tn), lambda l, j: (l, 0, j)),
        out_shape=jax.ShapeDtypeStruct((depth, b, n), F32),
        compiler_params=_cparams(("parallel", "parallel")),
        name="ada_mod",
    )(c, ada_w, ada_b.reshape(depth, 1, n))


def _proj_kernel(x_ref, mod_ref, w_ref, q_ref, u_ref, kc_ref, vc_ref, ks_ref, vs_ref, kw_ref,
                 vw_ref, g_ref):
    m = mod_ref[0]
    h = x_ref[0] * (1.0 + m[1:2]) + m[0:1]
    r = _dot(h.astype(BF16), w_ref[...])
    q_ref[0] = (r[:, 0:512] * (HEAD_DIM ** -0.5)).astype(BF16)
    u_ref[0] = r[:, 512:1024]
    kc_ref[0] = r[:, 1024:1152]
    vc_ref[0] = r[:, 1152:1280]
    g_ref[0] = r[:, 1792:1920]
    tm = r.shape[0]
    lane = lax.broadcasted_iota(jnp.int32, (tm, 128), 1)
    key = pl.program_id(1) * tm + lax.broadcasted_iota(jnp.int32, (tm, 128), 0)
    lo_half = lane < HEAD_DIM
    block_onehot = jnp.where((key >> 6) == lane - HEAD_DIM, 1.0, 0.0)
    ones_col = jnp.where(lane == HEAD_DIM, 1.0, 0.0)
    for ref, col, const in ((ks_ref, 1280, block_onehot), (vs_ref, 1408, ones_col),
                            (kw_ref, 1536, 0.0), (vw_ref, 1664, ones_col)):
        slab = r[:, col:col + 128]
        ref[0, 0] = jnp.where(lo_half, slab, const).astype(BF16)
        ref[0, 1] = jnp.where(lo_half, pltpu.roll(slab, HEAD_DIM, 1), const).astype(BF16)


def _in_proj(x, mod, w_pad):
    b, l, d = x.shape
    tm = 512
    row = lambda n: pl.BlockSpec((1, tm, n), lambda i, j: (i, j, 0))
    shp = lambda n, dt: jax.ShapeDtypeStruct((b, l, n), dt)
    heads = pl.BlockSpec((1, N_KV_HEADS, tm, 128), lambda i, j: (i, 0, j, 0))
    hshp = jax.ShapeDtypeStruct((b, N_KV_HEADS, l, 128), BF16)
    return pl.pallas_call(
        _proj_kernel,
        grid=(b, l // tm),
        in_specs=[row(d),
                  pl.BlockSpec((1, 6, d), lambda i, j: (i, 0, 0)),
                  pl.BlockSpec((d, PROJ_PAD), lambda i, j: (0, 0))],
        out_specs=[row(512), row(512), row(128), row(128), heads, heads, heads, heads, row(128)],
        out_shape=[shp(512, BF16), shp(512, F32), shp(128, F32), shp(128, F32), hshp, hshp, hshp,
                   hshp, shp(128, F32)],
        compiler_params=_cparams(("parallel", "parallel")),
        name="in_proj",
    )(x, mod, w_pad)


def _compress_one(kr, pos_a, pos_b, w1a, w1b, w2):
    nch = kr.shape[0]
    pa = _dot((kr + pos_a).astype(BF16), w1a)
    pb = _dot((kr + pos_b).astype(BF16), w1b)
    hid = _gelu(pa + pltpu.roll(pb, nch - 1, 0))
    out = _dot(hid.astype(BF16), w2)
    rows = lax.broadcasted_iota(jnp.int32, out.shape, 0)
    return jnp.where(rows < nch - 1, out, 0.0)


def _compress_kernel(k_ref, v_ref, pak_ref, pbk_ref, w1ak_ref, w1bk_ref, w2k_ref,
                     pav_ref, pbv_ref, w1av_ref, w1bv_ref, w2v_ref, ko_ref, vo_ref):
    k = _compress_one(k_ref[0], pak_ref[...], pbk_ref[...], w1ak_ref[...], w1bk_ref[...], w2k_ref[...])
    v = _compress_one(v_ref[0], pav_ref[...], pbv_ref[...], w1av_ref[...], w1bv_ref[...], w2v_ref[...])
    lo_half = lax.broadcasted_iota(jnp.int32, k.shape, 1) < HEAD_DIM
    for ref, val in ((ko_ref, k), (vo_ref, v)):
        ref[0, 0] = jnp.where(lo_half, val, 0.0).astype(BF16)
        ref[0, 1] = jnp.where(lo_half, pltpu.roll(val, HEAD_DIM, 1), 0.0).astype(BF16)


def _compress_weights(pos, w1, w2):
    eye = jnp.eye(N_KV_HEADS, dtype=F32)
    half = CMP_STRIDE

    def big(w):
        return jnp.einsum('sdf,hg->shdgf', w, eye).reshape(half * KV_WIDTH, KV_WIDTH).astype(BF16)

    def posrow(p):
        return jnp.broadcast_to(p[:, None, :], (half, N_KV_HEADS, HEAD_DIM)).reshape(1, half * KV_WIDTH)

    w2b = jnp.einsum('fd,hg->hfgd', w2, eye).reshape(KV_WIDTH, KV_WIDTH).astype(BF16)
    return posrow(pos[:half]), posrow(pos[half:]), big(w1[:half]), big(w1[half:]), w2b


def _compress(kc, vc, wk, wv):
    b, l, _ = kc.shape
    nch = l // CMP_STRIDE
    kr = kc.reshape(b, nch, CMP_STRIDE * KV_WIDTH)
    vr = vc.reshape(b, nch, CMP_STRIDE * KV_WIDTH)
    full = lambda a: pl.BlockSpec(a.shape, lambda i: (0,) * a.ndim)
    blk = pl.BlockSpec((1, nch, CMP_STRIDE * KV_WIDTH), lambda i: (i, 0, 0))
    oblk = pl.BlockSpec((1, N_KV_HEADS, nch, KV_WIDTH), lambda i: (i, 0, 0, 0))
    return pl.pallas_call(
        _compress_kernel,
        grid=(b,),
        in_specs=[blk, blk] + [full(a) for a in wk] + [full(a) for a in wv],
        out_specs=[oblk, oblk],
        out_shape=[jax.ShapeDtypeStruct((b, N_KV_HEADS, nch, KV_WIDTH), BF16)] * 2,
        compiler_params=_cparams(("parallel",)),
        name="compress",
    )(kr, vr, *wk, *wv)


def _attn_kernel(q_ref, g_ref, kc_ref, vc_ref, ks_ref, vs_ref, kw_ref, vw_ref, o_ref, *, seq):
    nch = seq // CMP_STRIDE
    n_cmp = nch - 1
    n_slc = seq // SLC_LEN
    n_sel = min(N_SEL, n_slc)
    qb, gq, hd = Q_BLOCK, Q_PER_KV, HEAD_DIM
    nb = HEAD_DIM
    i = pl.program_id(1)
    qs = i * qb
    t_col = qs + lax.broadcasted_iota(jnp.int32, (qb, 1), 0)
    t_row = qs + lax.broadcasted_iota(jnp.int32, (1, qb), 1)
    lo_half = lax.broadcasted_iota(jnp.int32, (1, 128), 1) < hd

    gates = _sigmoid(g_ref[0])

    q_heads = []
    for m in range(N_Q_HEADS // 2):
        slab = q_ref[0, :, 128 * m:128 * (m + 1)]
        q_heads.append(slab)
        q_heads.append(pltpu.roll(slab.astype(F32), hd, 1).astype(BF16))

    n_idx = lax.broadcasted_iota(jnp.int32, (1, nch), 1)
    cmp_valid = ((n_idx * CMP_STRIDE + (CMP_LEN - 1)) <= t_col) & (n_idx < n_cmp)
    cmp_bias = jnp.where(cmp_valid, 0.0, NEG)
    cmp_keep = cmp_valid.astype(F32)
    oj = lax.broadcasted_iota(jnp.int32, (nb, nch), 0) * SLC_LEN
    on = lax.broadcasted_iota(jnp.int32, (nb, nch), 1) * CMP_STRIDE
    overlap_t = ((on < oj + SLC_LEN) & (on + CMP_LEN > oj)).astype(F32)

    j_col = lax.broadcasted_iota(jnp.int32, (nb, 1), 0)
    cur = t_row >> 6
    causal_t = (j_col * SLC_LEN) <= t_row
    forced_t = (j_col == 0) | (j_col == cur) | (j_col == cur - 1)
    sub = lax.broadcasted_iota(jnp.int32, (8, 1), 0)

    heads = range(N_Q_HEADS)
    kv_heads = range(N_KV_HEADS)
    kcb = [kc_ref[0, h] for h in kv_heads]
    vcb = [vc_ref[0, h] for h in kv_heads]
    s_c = [_dot_nt(q_heads[hq], kcb[hq // gq]) + cmp_bias for hq in heads]
    e_c = [jnp.exp(s_c[hq] - jnp.max(s_c[hq], axis=-1, keepdims=True)) * cmp_keep for hq in heads]
    den = [jnp.sum(e_c[hq], axis=-1, keepdims=True) for hq in heads]
    p_c = [e_c[hq] * (1.0 / jnp.where(den[hq] > 0, den[hq], 1.0)) for hq in heads]
    o_cmp = [_dot(p_c[hq].astype(BF16), vcb[hq // gq]) for hq in heads]
    p_sum = [p_c[gq * h] + p_c[gq * h + 1] + p_c[gq * h + 2] + p_c[gq * h + 3] for h in kv_heads]
    imp_t = [_dot_nt(overlap_t, p_sum[h], lax.Precision.HIGHEST) for h in kv_heads]

    score_t = [jnp.where(causal_t, jnp.where(forced_t, jnp.inf, imp_t[h]), -jnp.inf)
               for h in kv_heads]
    n_vb = nb // 8
    blocks = [[score_t[h][8 * v:8 * (v + 1)] for v in range(n_vb)] for h in kv_heads]
    ranks = [[jnp.zeros((8, qb), F32) for _ in range(n_vb)] for h in kv_heads]
    for b in range(n_slc):
        for h in kv_heads:
            row = jnp.broadcast_to(score_t[h][b:b + 1], (8, qb))
            for v in range(n_vb):
                if 8 * v > b:
                    beats = row >= blocks[h][v]
                elif 8 * v + 7 < b:
                    beats = row > blocks[h][v]
                else:
                    beats = (row > blocks[h][v]) | ((row == blocks[h][v]) & (sub + 8 * v > b))
                ranks[h][v] = ranks[h][v] + jnp.where(beats, 1.0, 0.0)
    q_sel = []
    for h in kv_heads:
        rank_t = jnp.concatenate(ranks[h], axis=0)
        sel_neg_t = jnp.where((rank_t < n_sel) & causal_t, 0.0, NEG)
        sel_neg = jnp.concatenate([sel_neg_t, sel_neg_t], axis=0).T.astype(BF16)
        for g in range(gq):
            q_sel.append(jnp.where(lo_half, q_heads[gq * h + g], sel_neg))

    n_tot = (qs + qb + SLC_KEY_CHUNK - 1) // SLC_KEY_CHUNK
    last_k0 = (n_tot - 1) * SLC_KEY_CHUNK
    kpos = last_k0 + lax.broadcasted_iota(jnp.int32, (1, SLC_KEY_CHUNK), 1)
    diag_bias = jnp.where(kpos <= t_col, 0.0, NEG)

    def slc_chunk(k0, carry, bias):
        heads = range(N_Q_HEADS)
        kblk = [ks_ref[0, h, pl.ds(k0, SLC_KEY_CHUNK), :] for h in range(N_KV_HEADS)]
        vblk = [vs_ref[0, h, pl.ds(k0, SLC_KEY_CHUNK), :] for h in range(N_KV_HEADS)]
        s = [_dot_nt(q_sel[hq], kblk[hq // gq]) for hq in heads]
        if bias is not None:
            s = [x + bias for x in s]
        m_new = [jnp.maximum(carry[hq][0], jnp.max(s[hq], axis=-1, keepdims=True)) for hq in heads]
        p = [jnp.exp(s[hq] - m_new[hq]).astype(BF16) for hq in heads]
        pv = [_dot(p[hq], vblk[hq // gq]) for hq in heads]
        return tuple((m_new[hq], jnp.exp(carry[hq][0] - m_new[hq]) * carry[hq][1] + pv[hq])
                     for hq in heads)

    init = tuple((jnp.full((qb, 1), NEG, F32), jnp.zeros((qb, 128), F32))
                 for _ in range(N_Q_HEADS))
    carry = lax.fori_loop(
        0, n_tot - 1,
        lambda c, cr: slc_chunk(pl.multiple_of(c * SLC_KEY_CHUNK, SLC_KEY_CHUNK), cr, None), init)
    carry = slc_chunk(pl.multiple_of(last_k0, SLC_KEY_CHUNK), carry, diag_bias)
    o_slc = [acc for _, acc in carry]

    win_start = pl.multiple_of(jnp.maximum(qs - WINDOW, 0), qb)
    wpos = win_start + lax.broadcasted_iota(jnp.int32, (1, WINDOW + qb), 1)
    dist = t_col - wpos
    win_bias = jnp.where((dist >= 0) & (dist < WINDOW), 0.0, NEG)

    kwb = [kw_ref[0, h, pl.ds(win_start, WINDOW + qb), :] for h in kv_heads]
    vwb = [vw_ref[0, h, pl.ds(win_start, WINDOW + qb), :] for h in kv_heads]
    s_w = [_dot_nt(q_heads[hq], kwb[hq // gq]) + win_bias for hq in heads]
    e_w = [jnp.exp(s_w[hq] - jnp.max(s_w[hq], axis=-1, keepdims=True)).astype(BF16)
           for hq in heads]
    o_win = [_dot(e_w[hq], vwb[hq // gq]) for hq in heads]

    res = []
    for hq in range(N_Q_HEADS):
        c0 = 3 * hq
        w_slc = gates[:, c0 + 1:c0 + 2] / o_slc[hq][:, hd:hd + 1]
        w_win = gates[:, c0 + 2:c0 + 3] / o_win[hq][:, hd:hd + 1]
        res.append(gates[:, c0:c0 + 1] * o_cmp[hq] + w_slc * o_slc[hq] + w_win * o_win[hq])
    for m in range(N_Q_HEADS // 2):
        o_ref[0, :, 128 * m:128 * (m + 1)] = jnp.where(
            lo_half, res[2 * m], pltpu.roll(res[2 * m + 1], hd, 1))


def _attention(q, g, k_cmp, v_cmp, ks, vs, kw, vw):
    b, l, _ = q.shape
    nch = l // CMP_STRIDE
    assert l // SLC_LEN <= HEAD_DIM and l >= WINDOW + Q_BLOCK and l % SLC_KEY_CHUNK == 0
    tile = lambda n: pl.BlockSpec((1, Q_BLOCK, n), lambda i, j: (i, j, 0))
    whole = lambda r: pl.BlockSpec((1, N_KV_HEADS, r, KV_WIDTH), lambda i, j: (i, 0, 0, 0))
    return pl.pallas_call(
        functools.partial(_attn_kernel, seq=l),
        grid=(b, l // Q_BLOCK),
        in_specs=[tile(ATTN_WIDTH), tile(128), whole(nch), whole(nch), whole(l), whole(l),
                  whole(l), whole(l)],
        out_specs=tile(ATTN_WIDTH),
        out_shape=jax.ShapeDtypeStruct((b, l, ATTN_WIDTH), F32),
        compiler_params=_cparams(("parallel", "arbitrary")),
        name="nsa_attention",
    )(q, g, k_cmp, v_cmp, ks, vs, kw, vw)


def _ssm_param_kernel(are_ref, aim_ref, ldt_ref, btr_ref, bti_ref, cr_ref, ci_ref,
                      t_ref, bcr_ref, bci_ref, ccr_ref, cci_ref, a16_ref):
    tc, hg, p = SSM_CHUNK, SSM_GROUP, SSM_STATE
    rows = tc * hg
    a_re = are_ref[0]
    a_im = aim_ref[0]
    dt = jnp.exp(ldt_ref[0])
    lam_re = a_re * dt
    lam_im = a_im * dt

    def apow(k):
        mag = jnp.exp(k * lam_re)
        return mag * jnp.cos(k * lam_im), mag * jnp.sin(k * lam_im)

    one = jnp.ones((1, 1), F32)
    ab_re, ab_im = apow(one)
    nr, ni = ab_re - 1.0, ab_im
    inv = 1.0 / (a_re * a_re + a_im * a_im)
    cf_re = (nr * a_re + ni * a_im) * inv
    cf_im = (ni * a_re - nr * a_im) * inv
    bt_re, bt_im = btr_ref[0], bti_ref[0]
    bb_re = cf_re * bt_re - cf_im * bt_im
    bb_im = cf_re * bt_im + cf_im * bt_re
    bb_re = jnp.concatenate([bb_re] * tc, axis=0)
    bb_im = jnp.concatenate([bb_im] * tc, axis=0)
    c_re = jnp.concatenate([cr_ref[0]] * tc, axis=0)
    c_im = jnp.concatenate([ci_ref[0]] * tc, axis=0)
    step = (lax.broadcasted_iota(jnp.int32, (rows, 1), 0) >> 4).astype(F32)

    def cmul(xr, xi, yr, yi):
        return xr * yr - xi * yi, xr * yi + xi * yr

    l_re, l_im = cmul(bb_re, bb_im, *apow(-step))
    r_re, r_im = cmul(c_re, c_im, *apow(step))
    hp = lax.Precision.HIGHEST
    tm = _dot_nt(l_re, r_re, hp) - _dot_nt(l_im, r_im, hp)
    s_row = lax.broadcasted_iota(jnp.int32, (rows, rows), 0) >> 4
    t_col = lax.broadcasted_iota(jnp.int32, (rows, rows), 1) >> 4
    t_ref[0] = jnp.where(t_col >= s_row, tm, 0.0).astype(BF16)
    e_re, e_im = cmul(bb_re, bb_im, *apow((tc - 1.0) - step))
    bcr_ref[0] = e_re.astype(BF16)
    bci_ref[0] = e_im.astype(BF16)
    f_re, f_im = cmul(c_re, c_im, *apow(step + 1.0))
    ccr_ref[0] = f_re.astype(BF16)
    cci_ref[0] = (-f_im).astype(BF16)
    p_re, p_im = apow(float(tc) * one)
    a16_ref[0] = jnp.concatenate([p_re, p_im], axis=0)


def _ssm_params(a_re, a_im, log_dt, b_re, b_im, c_re, c_im):
    g, p = a_re.shape
    hg, tc = SSM_GROUP, SSM_CHUNK
    rows = tc * hg
    vec = pl.BlockSpec((1, 1, p), lambda i: (i, 0, 0))
    mat = pl.BlockSpec((1, hg, p), lambda i: (i, 0, 0))
    out_rp = pl.BlockSpec((1, rows, p), lambda i: (i, 0, 0))
    return pl.pallas_call(
        _ssm_param_kernel,
        grid=(g,),
        in_specs=[vec, vec, pl.BlockSpec((1, 1, 1), lambda i: (i, 0, 0)), mat, mat, mat, mat],
        out_specs=[pl.BlockSpec((1, rows, rows), lambda i: (i, 0, 0)), out_rp, out_rp, out_rp, out_rp,
                   pl.BlockSpec((1, 2, p), lambda i: (i, 0, 0))],
        out_shape=[jax.ShapeDtypeStruct((g, rows, rows), BF16)] +
                  [jax.ShapeDtypeStruct((g, rows, p), BF16)] * 4 +
                  [jax.ShapeDtypeStruct((g, 2, p), F32)],
        compiler_params=_cparams(("parallel",)),
        name="ssm_params",
    )(a_re.reshape(g, 1, p), a_im.reshape(g, 1, p), log_dt.reshape(g, 1, 1),
      jnp.swapaxes(b_re, 1, 2), jnp.swapaxes(b_im, 1, 2), c_re, c_im)


def _ssm_kernel(u_ref, t_ref, bcr_ref, bci_ref, ccr_ref, cci_ref, a16_ref, d_ref, y_ref,
                wre, wim, xre, xim, *, batch, n_chunks):
    u = u_ref[0]
    ub = u.astype(BF16)
    wre[...] = _dot(ub, bcr_ref[0])
    wim[...] = _dot(ub, bci_ref[0])
    a16 = a16_ref[0]
    a_r = a16[0:1]
    a_i = a16[1:2]

    def step(c, carry):
        x_r, x_i = carry
        r0 = c * batch
        if batch % 8 == 0:
            r0 = pl.multiple_of(r0, 8)
        xre[pl.ds(r0, batch), :] = x_r
        xim[pl.ds(r0, batch), :] = x_i
        w_r = wre[pl.ds(r0, batch), :]
        w_i = wim[pl.ds(r0, batch), :]
        return a_r * x_r - a_i * x_i + w_r, a_r * x_i + a_i * x_r + w_i

    zero = jnp.zeros((batch, SSM_STATE), F32)
    lax.fori_loop(0, n_chunks, step, (zero, zero))
    y = _dot(ub, t_ref[0])
    y = y + _dot_nt(xre[...].astype(BF16), ccr_ref[0]) + _dot_nt(xim[...].astype(BF16), cci_ref[0])
    y_ref[0] = y + d_ref[0] * u


def _ssm(u, params, d_skip):
    b, l, _ = u.shape
    g, hg, tc, p = N_SSM_GROUPS, SSM_GROUP, SSM_CHUNK, SSM_STATE
    nch = l // tc
    rows = nch * b
    w = tc * hg
    ug = u.reshape(b, nch, tc, g, hg).transpose(3, 1, 0, 2, 4).reshape(g, rows, w)
    d_t = jnp.broadcast_to(d_skip[:, None, :], (g, tc, hg)).reshape(g, 1, w)
    t_op, bcr, bci, ccr, cci, a16 = params
    grp = lambda r, c: pl.BlockSpec((1, r, c), lambda i: (i, 0, 0))
    y = pl.pallas_call(
        functools.partial(_ssm_kernel, batch=b, n_chunks=nch),
        grid=(g,),
        in_specs=[grp(rows, w), grp(w, w), grp(w, p), grp(w, p), grp(w, p), grp(w, p), grp(2, p),
                  grp(1, w)],
        out_specs=grp(rows, w),
        out_shape=jax.ShapeDtypeStruct((g, rows, w), F32),
        scratch_shapes=[pltpu.VMEM((rows, p), F32)] * 4,
        compiler_params=_cparams(("parallel",)),
        name="ssm_scan",
    )(ug, t_op, bcr, bci, ccr, cci, a16, d_t)
    return y.reshape(g, nch, b, tc, hg).transpose(2, 1, 3, 0, 4).reshape(b, l, g * hg)


def _mix_out_kernel(oa_ref, y_ref, x_ref, mod_ref, wglu_ref, woa_ref, wos_ref, na_ref, ns_ref,
                    lg_ref, lb_ref, o_ref):
    m = mod_ref[0]
    ya = _gelu(y_ref[0])
    o_ssm = ya * _sigmoid(_dot(ya.astype(BF16), wglu_ref[...]))
    ra = _rms_norm(oa_ref[0], na_ref[...])
    rs = _rms_norm(o_ssm, ns_ref[...])
    mix = _dot(ra.astype(BF16), woa_ref[...]) + _dot(rs.astype(BF16), wos_ref[...])
    o_ref[0] = _layer_norm(DEEPNORM_ALPHA * x_ref[0] + (1.0 + m[2:3]) * mix, lg_ref[...], lb_ref[...])


def _mix_out(o_attn, y, x, mod, w_glu, w_out_a, w_out_s, norm_attn, norm_ssm, ln_g, ln_b):
    b, l, d = x.shape
    tm = 512
    row = lambda n: pl.BlockSpec((1, tm, n), lambda i, j: (i, j, 0))
    full = lambda a: pl.BlockSpec(a.shape, lambda i, j: (0,) * a.ndim)
    consts = (w_glu, w_out_a, w_out_s, norm_attn, norm_ssm, ln_g, ln_b)
    return pl.pallas_call(
        _mix_out_kernel,
        grid=(b, l // tm),
        in_specs=[row(ATTN_WIDTH), row(SSM_WIDTH), row(d),
                  pl.BlockSpec((1, 6, d), lambda i, j: (i, 0, 0))] + [full(a) for a in consts],
        out_specs=row(d),
        out_shape=jax.ShapeDtypeStruct((b, l, d), F32),
        compiler_params=_cparams(("parallel", "parallel")),
        name="mix_out",
    )(o_attn, y, x, mod, *consts)


FF_CHUNK = 1408


def _ffn_kernel(x_ref, mod_ref, wg_ref, wu_ref, wd_ref, lg_ref, lb_ref, o_ref, h_scr, acc):
    j = pl.program_id(2)
    m = mod_ref[0]

    @pl.when(j == 0)
    def _():
        h_scr[...] = (x_ref[0] * (1.0 + m[4:5]) + m[3:4]).astype(BF16)
        acc[...] = jnp.zeros_like(acc)

    h = h_scr[...]
    gate = _dot(h, wg_ref[...])
    up = _dot(h, wu_ref[...])
    act = (gate * _sigmoid(gate) * up).astype(BF16)
    acc[...] += _dot(act, wd_ref[...])

    @pl.when(j == pl.num_programs(2) - 1)
    def _():
        o_ref[0] = _layer_norm(DEEPNORM_ALPHA * x_ref[0] + (1.0 + m[5:6]) * acc[...],
                               lg_ref[...], lb_ref[...])


def _ffn(x, mod, wg, wu, wd, ln_g, ln_b):
    b, l, d = x.shape
    tm = 512
    nf = D_FF // FF_CHUNK
    return pl.pallas_call(
        _ffn_kernel,
        grid=(b, l // tm, nf),
        in_specs=[pl.BlockSpec((1, tm, d), lambda i, r, j: (i, r, 0)),
                  pl.BlockSpec((1, 6, d), lambda i, r, j: (i, 0, 0)),
                  pl.BlockSpec((d, FF_CHUNK), lambda i, r, j: (0, j)),
                  pl.BlockSpec((d, FF_CHUNK), lambda i, r, j: (0, j)),
                  pl.BlockSpec((FF_CHUNK, d), lambda i, r, j: (j, 0)),
                  pl.BlockSpec((1, d), lambda i, r, j: (0, 0)),
                  pl.BlockSpec((1, d), lambda i, r, j: (0, 0))],
        out_specs=pl.BlockSpec((1, tm, d), lambda i, r, j: (i, r, 0)),
        out_shape=jax.ShapeDtypeStruct((b, l, d), F32),
        scratch_shapes=[pltpu.VMEM((tm, d), BF16), pltpu.VMEM((tm, d), F32)],
        compiler_params=_cparams(("parallel", "parallel", "arbitrary")),
        name="ffn",
    )(x, mod, wg, wu, wd, ln_g, ln_b)


def _moe_kernel(x_ref, mod_ref, rt_ref, wg_ref, wu_ref, wd_ref, lg_ref, lb_ref, o_ref,
                h_scr, gate_scr, acc):
    e = pl.program_id(2)
    j = pl.program_id(3)
    m = mod_ref[0]
    lane = lax.broadcasted_iota(jnp.int32, gate_scr.shape, 1)

    @pl.when((e == 0) & (j == 0))
    def _():
        h = x_ref[0] * (1.0 + m[4:5]) + m[3:4]
        h_scr[...] = h.astype(BF16)
        acc[...] = jnp.zeros_like(acc)
        logits = jnp.dot(h, rt_ref[...], preferred_element_type=F32, precision=lax.Precision.HIGHEST)
        logits = jnp.where(lane < N_EXPERTS, logits, -jnp.inf)
        lane_f = lane.astype(F32)
        v1 = jnp.max(logits, axis=-1, keepdims=True)
        i1 = jnp.min(jnp.where(logits == v1, lane_f, 128.0), axis=-1, keepdims=True)
        rest = jnp.where(lane_f == i1, -jnp.inf, logits)
        v2 = jnp.max(rest, axis=-1, keepdims=True)
        i2 = jnp.min(jnp.where(rest == v2, lane_f, 128.0), axis=-1, keepdims=True)
        e2 = jnp.exp(v2 - v1)
        den = 1.0 + e2
        gate_scr[...] = (jnp.where(lane_f == i1, 1.0 / den, 0.0)
                         + jnp.where(lane_f == i2, e2 / den, 0.0))

    h = h_scr[...]
    gate = _dot(h, wg_ref[0])
    up = _dot(h, wu_ref[0])
    act = (gate * _sigmoid(gate) * up).astype(BF16)
    g_e = jnp.sum(jnp.where(lane == e, gate_scr[...], 0.0), axis=-1, keepdims=True)
    acc[...] += g_e * _dot(act, wd_ref[0])

    @pl.when((e == pl.num_programs(2) - 1) & (j == pl.num_programs(3) - 1))
    def _():
        o_ref[0] = _layer_norm(DEEPNORM_ALPHA * x_ref[0] + (1.0 + m[5:6]) * acc[...],
                               lg_ref[...], lb_ref[...])


def _moe(x, mod, router_pad, wg, wu, wd, ln_g, ln_b):
    b, l, d = x.shape
    tm = 512
    nf = D_FF // FF_CHUNK
    return pl.pallas_call(
        _moe_kernel,
        grid=(b, l // tm, N_EXPERTS, nf),
        in_specs=[pl.BlockSpec((1, tm, d), lambda i, r, e, j: (i, r, 0)),
                  pl.BlockSpec((1, 6, d), lambda i, r, e, j: (i, 0, 0)),
                  pl.BlockSpec((d, 128), lambda i, r, e, j: (0, 0)),
                  pl.BlockSpec((1, d, FF_CHUNK), lambda i, r, e, j: (e, 0, j)),
                  pl.BlockSpec((1, d, FF_CHUNK), lambda i, r, e, j: (e, 0, j)),
                  pl.BlockSpec((1, FF_CHUNK, d), lambda i, r, e, j: (e, j, 0)),
                  pl.BlockSpec((1, d), lambda i, r, e, j: (0, 0)),
                  pl.BlockSpec((1, d), lambda i, r, e, j: (0, 0))],
        out_specs=pl.BlockSpec((1, tm, d), lambda i, r, e, j: (i, r, 0)),
        out_shape=jax.ShapeDtypeStruct((b, l, d), F32),
        scratch_shapes=[pltpu.VMEM((tm, d), BF16), pltpu.VMEM((tm, 128), F32),
                        pltpu.VMEM((tm, d), F32)],
        compiler_params=_cparams(("parallel", "parallel", "arbitrary", "arbitrary")),
        name="moe",
    )(x, mod, router_pad, wg, wu, wd, ln_g, ln_b)


def _reorder_w_in(w):
    q = w[:, :ATTN_WIDTH]
    kv = w[:, ATTN_WIDTH:ATTN_WIDTH + 6 * KV_WIDTH]
    g0 = ATTN_WIDTH + 6 * KV_WIDTH
    gates = w[:, g0:g0 + 24]
    u = w[:, g0 + 24:]
    pad = jnp.zeros((w.shape[0], PROJ_PAD - w.shape[1]), w.dtype)
    return jnp.concatenate([q, u, kv, gates, pad], axis=1).astype(BF16)


def kernel(x, c, w_in, cmp_pos_k, cmp_pos_v, cmp_w1_k, cmp_w2_k, cmp_w1_v, cmp_w2_v, ssm_a_re,
           ssm_a_im, ssm_log_dt, ssm_b_re, ssm_b_im, ssm_c_re, ssm_c_im, ssm_d, ssm_w_glu, norm_attn,
           norm_ssm, w_out, ada_w, ada_b, ln_g, ln_b, ffn_w_gate, ffn_w_up, ffn_w_down, moe_router,
           moe_w_gate, moe_w_up, moe_w_down):
    b, l, d = x.shape
    mod_all = _ada_mod(c, ada_w, ada_b).reshape(DEPTH, b, 6, d)
    for layer in range(DEPTH):
        mod = mod_all[layer]
        q, u, kc, vc, ks, vs, kw, vw, g = _in_proj(x, mod, _reorder_w_in(w_in[layer]))
        k_cmp, v_cmp = _compress(
            kc, vc,
            _compress_weights(cmp_pos_k[layer], cmp_w1_k[layer], cmp_w2_k[layer]),
            _compress_weights(cmp_pos_v[layer], cmp_w1_v[layer], cmp_w2_v[layer]))
        o_attn = _attention(q, g, k_cmp, v_cmp, ks, vs, kw, vw)
        params = _ssm_params(ssm_a_re[layer], ssm_a_im[layer], ssm_log_dt[layer], ssm_b_re[layer],
                             ssm_b_im[layer], ssm_c_re[layer], ssm_c_im[layer])
        y = _ssm(u, params, ssm_d[layer])
        wo = w_out[layer].astype(BF16)
        x = _mix_out(o_attn, y, x, mod, ssm_w_glu[layer].astype(BF16), wo[:ATTN_WIDTH],
                     wo[ATTN_WIDTH:], norm_attn[layer][None], norm_ssm[layer][None],
                     ln_g[layer, 0][None], ln_b[layer, 0][None])
        lg, lb = ln_g[layer, 1][None], ln_b[layer, 1][None]
        if layer % 2 == 0:
            x = _ffn(x, mod, ffn_w_gate[layer // 2].astype(BF16), ffn_w_up[layer // 2].astype(BF16),
                     ffn_w_down[layer // 2].astype(BF16), lg, lb)
        else:
            rt = jnp.pad(moe_router[layer // 2], ((0, 0), (0, 128 - N_EXPERTS)))
            x = _moe(x, mod, rt, moe_w_gate[layer // 2].astype(BF16),
                     moe_w_up[layer // 2].astype(BF16), moe_w_down[layer // 2].astype(BF16), lg, lb)
    return x
```

```python
import functools
import math

import jax
import jax.numpy as jnp
from jax import lax
from jax.experimental import pallas as pl
from jax.experimental.pallas import tpu as pltpu

F32 = jnp.float32
BF16 = jnp.bfloat16

D_MODEL = 1024
DEPTH = 4
HEAD_DIM = 64
N_Q_HEADS = 8
N_KV_HEADS = 2
Q_PER_KV = 4
ATTN_WIDTH = 512
KV_WIDTH = 128
SSM_WIDTH = 512
SSM_GROUP = 16
N_SSM_GROUPS = 32
SSM_STATE = 64
CMP_LEN = 32
CMP_STRIDE = 16
SLC_LEN = 64
N_SEL = 16
WINDOW = 512
Q_BLOCK = 128
D_FF = 2816
N_EXPERTS = 8
DEEPNORM_ALPHA = (2.0 * DEPTH) ** 0.25
LN_EPS = 1e-5
RMS_EPS = 1e-6

PROJ_PAD = 1920
SSM_CHUNK = 16
SLC_KEY_CHUNK = 512
NEG = -1e30
VMEM_LIMIT = 56 * 1024 * 1024


def _cparams(sem):
    return pltpu.CompilerParams(dimension_semantics=sem, vmem_limit_bytes=VMEM_LIMIT)


def _sigmoid(x):
    return 1.0 / (1.0 + jnp.exp(-x))


def _gelu(x):
    return 0.5 * x * (1.0 + jnp.tanh(math.sqrt(2.0 / math.pi) * (x + 0.044715 * (x * x * x))))


def _layer_norm(x, g, b):
    mu = jnp.mean(x, axis=-1, keepdims=True)
    xc = x - mu
    var = jnp.mean(xc * xc, axis=-1, keepdims=True)
    return xc * lax.rsqrt(var + LN_EPS) * g + b


def _rms_norm(x, g):
    return x * lax.rsqrt(jnp.mean(x * x, axis=-1, keepdims=True) + RMS_EPS) * g


def _dot(a, b):
    return jnp.dot(a, b, preferred_element_type=F32)


def _dot_nt(a, b, precision=None):
    return lax.dot_general(a, b, (((1,), (1,)), ((), ())), preferred_element_type=F32,
                           precision=precision)


def _ada_kernel(c_ref, w_ref, b_ref, o_ref):
    c = c_ref[...]
    ca = (c * _sigmoid(c)).astype(BF16)
    o_ref[0] = _dot(ca, w_ref[0].astype(BF16)) + b_ref[0]


def _ada_mod(c, ada_w, ada_b):
    depth, d, n = ada_w.shape
    b = c.shape[0]
    tn = 1536
    return pl.pallas_call(
        _ada_kernel,
        grid=(depth, n // tn),
        in_specs=[pl.BlockSpec((b, d), lambda l, j: (0, 0)),
                  pl.BlockSpec((1, d, tn), lambda l, j: (l, 0, j)),
                  pl.BlockSpec((1, 1, tn), lambda l, j: (l, 0, j))],
        out_specs=pl.BlockSpec((1, b, tn), lambda l, j: (l, 0, j)),
        out_shape=jax.ShapeDtypeStruct((depth, b, n), F32),
        compiler_params=_cparams(("parallel", "parallel")),
        name="ada_mod",
    )(c, ada_w, ada_b.reshape(depth, 1, n))


def _proj_kernel(x_ref, mod_ref, w_ref, q_ref, u_ref, kc_ref, vc_ref, ks_ref, vs_ref, kw_ref,
                 vw_ref, g_ref):
    m = mod_ref[0]
    h = x_ref[0] * (1.0 + m[1:2]) + m[0:1]
    r = _dot(h.astype(BF16), w_ref[...])
    q_ref[0] = (r[:, 0:512] * (HEAD_DIM ** -0.5)).astype(BF16)
    u_ref[0] = r[:, 512:1024]
    kc_ref[0] = r[:, 1024:1152]
    vc_ref[0] = r[:, 1152:1280]
    g_ref[0] = r[:, 1792:1920]
    tm = r.shape[0]
    lane = lax.broadcasted_iota(jnp.int32, (tm, 128), 1)
    key = pl.program_id(1) * tm + lax.broadcasted_iota(jnp.int32, (tm, 128), 0)
    lo_half = lane < HEAD_DIM
    block_onehot = jnp.where((key >> 6) == lane - HEAD_DIM, 1.0, 0.0)
    ones_col = jnp.where(lane == HEAD_DIM, 1.0, 0.0)
    for ref, col, const in ((ks_ref, 1280, block_onehot), (vs_ref, 1408, ones_col),
                            (kw_ref, 1536, 0.0), (vw_ref, 1664, ones_col)):
        slab = r[:, col:col + 128]
        ref[0, 0] = jnp.where(lo_half, slab, const).astype(BF16)
        ref[0, 1] = jnp.where(lo_half, pltpu.roll(slab, HEAD_DIM, 1), const).astype(BF16)


def _in_proj(x, mod, w_pad):
    b, l, d = x.shape
    tm = 512
    row = lambda n: pl.BlockSpec((1, tm, n), lambda i, j: (i, j, 0))
    shp = lambda n, dt: jax.ShapeDtypeStruct((b, l, n), dt)
    heads = pl.BlockSpec((1, N_KV_HEADS, tm, 128), lambda i, j: (i, 0, j, 0))
    hshp = jax.ShapeDtypeStruct((b, N_KV_HEADS, l, 128), BF16)
    return pl.pallas_call(
        _proj_kernel,
        grid=(b, l // tm),
        in_specs=[row(d),
                  pl.BlockSpec((1, 6, d), lambda i, j: (i, 0, 0)),
                  pl.BlockSpec((d, PROJ_PAD), lambda i, j: (0, 0))],
        out_specs=[row(512), row(512), row(128), row(128), heads, heads, heads, heads, row(128)],
        out_shape=[shp(512, BF16), shp(512, F32), shp(128, F32), shp(128, F32), hshp, hshp, hshp,
                   hshp, shp(128, F32)],
        compiler_params=_cparams(("parallel", "parallel")),
        name="in_proj",
    )(x, mod, w_pad)


def _compress_one(kr, pos_a, pos_b, w1a, w1b, w2):
    nch = kr.shape[0]
    pa = _dot((kr + pos_a).astype(BF16), w1a)
    pb = _dot((kr + pos_b).astype(BF16), w1b)
    hid = _gelu(pa + pltpu.roll(pb, nch - 1, 0))
    out = _dot(hid.astype(BF16), w2)
    rows = lax.broadcasted_iota(jnp.int32, out.shape, 0)
    return jnp.where(rows < nch - 1, out, 0.0)


def _compress_kernel(k_ref, v_ref, pak_ref, pbk_ref, w1ak_ref, w1bk_ref, w2k_ref,
                     pav_ref, pbv_ref, w1av_ref, w1bv_ref, w2v_ref, ko_ref, vo_ref):
    k = _compress_one(k_ref[0], pak_ref[...], pbk_ref[...], w1ak_ref[...], w1bk_ref[...], w2k_ref[...])
    v = _compress_one(v_ref[0], pav_ref[...], pbv_ref[...], w1av_ref[...], w1bv_ref[...], w2v_ref[...])
    lo_half = lax.broadcasted_iota(jnp.int32, k.shape, 1) < HEAD_DIM
    for ref, val in ((ko_ref, k), (vo_ref, v)):
        ref[0, 0] = jnp.where(lo_half, val, 0.0).astype(BF16)
        ref[0, 1] = jnp.where(lo_half, pltpu.roll(val, HEAD_DIM, 1), 0.0).astype(BF16)


def _compress_weights(pos, w1, w2):
    eye = jnp.eye(N_KV_HEADS, dtype=F32)
    half = CMP_STRIDE

    def big(w):
        return jnp.einsum('sdf,hg->shdgf', w, eye).reshape(half * KV_WIDTH, KV_WIDTH).astype(BF16)

    def posrow(p):
        return jnp.broadcast_to(p[:, None, :], (half, N_KV_HEADS, HEAD_DIM)).reshape(1, half * KV_WIDTH)

    w2b = jnp.einsum('fd,hg->hfgd', w2, eye).reshape(KV_WIDTH, KV_WIDTH).astype(BF16)
    return posrow(pos[:half]), posrow(pos[half:]), big(w1[:half]), big(w1[half:]), w2b


def _compress(kc, vc, wk, wv):
    b, l, _ = kc.shape
    nch = l // CMP_STRIDE
    kr = kc.reshape(b, nch, CMP_STRIDE * KV_WIDTH)
    vr = vc.reshape(b, nch, CMP_STRIDE * KV_WIDTH)
    full = lambda a: pl.BlockSpec(a.shape, lambda i: (0,) * a.ndim)
    blk = pl.BlockSpec((1, nch, CMP_STRIDE * KV_WIDTH), lambda i: (i, 0, 0))
    oblk = pl.BlockSpec((1, N_KV_HEADS, nch, KV_WIDTH), lambda i: (i, 0, 0, 0))
    return pl.pallas_call(
        _compress_kernel,
        grid=(b,),
        in_specs=[blk, blk] + [full(a) for a in wk] + [full(a) for a in wv],
        out_specs=[oblk, oblk],
        out_shape=[jax.ShapeDtypeStruct((b, N_KV_HEADS, nch, KV_WIDTH), BF16)] * 2,
        compiler_params=_cparams(("parallel",)),
        name="compress",
    )(kr, vr, *wk, *wv)


def _attn_kernel(q_ref, g_ref, kc_ref, vc_ref, ks_ref, vs_ref, kw_ref, vw_ref, o_ref, *, seq):
    nch = seq // CMP_STRIDE
    n_cmp = nch - 1
    n_slc = seq // SLC_LEN
    n_sel = min(N_SEL, n_slc)
    qb, gq, hd = Q_BLOCK, Q_PER_KV, HEAD_DIM
    nb = HEAD_DIM
    i = pl.program_id(1)
    qs = i * qb
    t_col = qs + lax.broadcasted_iota(jnp.int32, (qb, 1), 0)
    t_row = qs + lax.broadcasted_iota(jnp.int32, (1, qb), 1)
    lo_half = lax.broadcasted_iota(jnp.int32, (1, 128), 1) < hd

    gates = _sigmoid(g_ref[0])

    q_heads = []
    for m in range(N_Q_HEADS // 2):
        slab = q_ref[0, :, 128 * m:128 * (m + 1)]
        q_heads.append(slab)
        q_heads.append(pltpu.roll(slab.astype(F32), hd, 1).astype(BF16))

    n_idx = lax.broadcasted_iota(jnp.int32, (1, nch), 1)
    cmp_valid = ((n_idx * CMP_STRIDE + (CMP_LEN - 1)) <= t_col) & (n_idx < n_cmp)
    cmp_bias = jnp.where(cmp_valid, 0.0, NEG)
    cmp_keep = cmp_valid.astype(F32)
    oj = lax.broadcasted_iota(jnp.int32, (nb, nch), 0) * SLC_LEN
    on = lax.broadcasted_iota(jnp.int32, (nb, nch), 1) * CMP_STRIDE
    overlap_t = ((on < oj + SLC_LEN) & (on + CMP_LEN > oj)).astype(F32)

    j_col = lax.broadcasted_iota(jnp.int32, (nb, 1), 0)
    cur = t_row >> 6
    causal_t = (j_col * SLC_LEN) <= t_row
    forced_t = (j_col == 0) | (j_col == cur) | (j_col == cur - 1)
    sub = lax.broadcasted_iota(jnp.int32, (8, 1), 0)

    heads = range(N_Q_HEADS)
    kv_heads = range(N_KV_HEADS)
    kcb = [kc_ref[0, h] for h in kv_heads]
    vcb = [vc_ref[0, h] for h in kv_heads]
    s_c = [_dot_nt(q_heads[hq], kcb[hq // gq]) + cmp_bias for hq in heads]
    e_c = [jnp.exp(s_c[hq] - jnp.max(s_c[hq], axis=-1, keepdims=True)) * cmp_keep for hq in heads]
    den = [jnp.sum(e_c[hq], axis=-1, keepdims=True) for hq in heads]
    p_c = [e_c[hq] * (1.0 / jnp.where(den[hq] > 0, den[hq], 1.0)) for hq in heads]
    o_cmp = [_dot(p_c[hq].astype(BF16), vcb[hq // gq]) for hq in heads]
    p_sum = [p_c[gq * h] + p_c[gq * h + 1] + p_c[gq * h + 2] + p_c[gq * h + 3] for h in kv_heads]
    imp_t = [_dot_nt(overlap_t, p_sum[h], lax.Precision.HIGHEST) for h in kv_heads]

    score_t = [jnp.where(causal_t, jnp.where(forced_t, jnp.inf, imp_t[h]), -jnp.inf)
               for h in kv_heads]
    n_vb = nb // 8
    blocks = [[score_t[h][8 * v:8 * (v + 1)] for v in range(n_vb)] for h in kv_heads]
    ranks = [[jnp.zeros((8, qb), F32) for _ in range(n_vb)] for h in kv_heads]
    for b in range(n_slc):
        for h in kv_heads:
            row = jnp.broadcast_to(score_t[h][b:b + 1], (8, qb))
            for v in range(n_vb):
                if 8 * v > b:
                    beats = row >= blocks[h][v]
                elif 8 * v + 7 < b:
                    beats = row > blocks[h][v]
                else:
                    beats = (row > blocks[h][v]) | ((row == blocks[h][v]) & (sub + 8 * v > b))
                ranks[h][v] = ranks[h][v] + jnp.where(beats, 1.0, 0.0)
    q_sel = []
    for h in kv_heads:
        rank_t = jnp.concatenate(ranks[h], axis=0)
        sel_neg_t = jnp.where((rank_t < n_sel) & causal_t, 0.0, NEG)
        sel_neg = jnp.concatenate([sel_neg_t, sel_neg_t], axis=0).T.astype(BF16)
        for g in range(gq):
            q_sel.append(jnp.where(lo_half, q_heads[gq * h + g], sel_neg))

    n_tot = (qs + qb + SLC_KEY_CHUNK - 1) // SLC_KEY_CHUNK
    last_k0 = (n_tot - 1) * SLC_KEY_CHUNK
    kpos = last_k0 + lax.broadcasted_iota(jnp.int32, (1, SLC_KEY_CHUNK), 1)
    diag_bias = jnp.where(kpos <= t_col, 0.0, NEG)

    def slc_chunk(k0, carry, bias):
        heads = range(N_Q_HEADS)
        kblk = [ks_ref[0, h, pl.ds(k0, SLC_KEY_CHUNK), :] for h in range(N_KV_HEADS)]
        vblk = [vs_ref[0, h, pl.ds(k0, SLC_KEY_CHUNK), :] for h in range(N_KV_HEADS)]
        s = [_dot_nt(q_sel[hq], kblk[hq // gq]) for hq in heads]
        if bias is not None:
            s = [x + bias for x in s]
        m_new = [jnp.maximum(carry[hq][0], jnp.max(s[hq], axis=-1, keepdims=True)) for hq in heads]
        p = [jnp.exp(s[hq] - m_new[hq]).astype(BF16) for hq in heads]
        pv = [_dot(p[hq], vblk[hq // gq]) for hq in heads]
        return tuple((m_new[hq], jnp.exp(carry[hq][0] - m_new[hq]) * carry[hq][1] + pv[hq])
                     for hq in heads)

    init = tuple((jnp.full((qb, 1), NEG, F32), jnp.zeros((qb, 128), F32))
                 for _ in range(N_Q_HEADS))
    carry = lax.fori_loop(
        0, n_tot - 1,
        lambda c, cr: slc_chunk(pl.multiple_of(c * SLC_KEY_CHUNK, SLC_KEY_CHUNK), cr, None), init)
    carry = slc_chunk(pl.multiple_of(last_k0, SLC_KEY_CHUNK), carry, diag_bias)
    o_slc = [acc for _, acc in carry]

    win_start = pl.multiple_of(jnp.maximum(qs - WINDOW, 0), qb)
    wpos = win_start + lax.broadcasted_iota(jnp.int32, (1, WINDOW + qb), 1)
    dist = t_col - wpos
    win_bias = jnp.where((dist >= 0) & (dist < WINDOW), 0.0, NEG)

    kwb = [kw_ref[0, h, pl.ds(win_start, WINDOW + qb), :] for h in kv_heads]
    vwb = [vw_ref[0, h, pl.ds(win_start, WINDOW + qb), :] for h in kv_heads]
    s_w = [_dot_nt(q_heads[hq], kwb[hq // gq]) + win_bias for hq in heads]
    e_w = [jnp.exp(s_w[hq] - jnp.max(s_w[hq], axis=-1, keepdims=True)).astype(BF16)
           for hq in heads]
    o_win = [_dot(e_w[hq], vwb[hq // gq]) for hq in heads]

    res = []
    for hq in range(N_Q_HEADS):
        c0 = 3 * hq
        w_slc = gates[:, c0 + 1:c0 + 2] / o_slc[hq][:, hd:hd + 1]
        w_win = gates[:, c0 + 2:c0 + 3] / o_win[hq][:, hd:hd + 1]
        res.append(gates[:, c0:c0 + 1] * o_cmp[hq] + w_slc * o_slc[hq] + w_win * o_win[hq])
    for m in range(N_Q_HEADS // 2):
        o_ref[0, :, 128 * m:128 * (m + 1)] = jnp.where(
            lo_half, res[2 * m], pltpu.roll(res[2 * m + 1], hd, 1))


def _attention(q, g, k_cmp, v_cmp, ks, vs, kw, vw):
    b, l, _ = q.shape
    nch = l // CMP_STRIDE
    assert l // SLC_LEN <= HEAD_DIM and l >= WINDOW + Q_BLOCK and l % SLC_KEY_CHUNK == 0
    tile = lambda n: pl.BlockSpec((1, Q_BLOCK, n), lambda i, j: (i, j, 0))
    whole = lambda r: pl.BlockSpec((1, N_KV_HEADS, r, KV_WIDTH), lambda i, j: (i, 0, 0, 0))
    return pl.pallas_call(
        functools.partial(_attn_kernel, seq=l),
        grid=(b, l // Q_BLOCK),
        in_specs=[tile(ATTN_WIDTH), tile(128), whole(nch), whole(nch), whole(l), whole(l),
                  whole(l), whole(l)],
        out_specs=tile(ATTN_WIDTH),
        out_shape=jax.ShapeDtypeStruct((b, l, ATTN_WIDTH), F32),
        compiler_params=_cparams(("parallel", "arbitrary")),
        name="nsa_attention",
    )(q, g, k_cmp, v_cmp, ks, vs, kw, vw)


def _ssm_param_kernel(are_ref, aim_ref, ldt_ref, btr_ref, bti_ref, cr_ref, ci_ref,
                      t_ref, bcr_ref, bci_ref, ccr_ref, cci_ref, a16_ref):
    tc, hg, p = SSM_CHUNK, SSM_GROUP, SSM_STATE
    rows = tc * hg
    a_re = are_ref[0]
    a_im = aim_ref[0]
    dt = jnp.exp(ldt_ref[0])
    lam_re = a_re * dt
    lam_im = a_im * dt

    def apow(k):
        mag = jnp.exp(k * lam_re)
        return mag * jnp.cos(k * lam_im), mag * jnp.sin(k * lam_im)

    one = jnp.ones((1, 1), F32)
    ab_re, ab_im = apow(one)
    nr, ni = ab_re - 1.0, ab_im
    inv = 1.0 / (a_re * a_re + a_im * a_im)
    cf_re = (nr * a_re + ni * a_im) * inv
    cf_im = (ni * a_re - nr * a_im) * inv
    bt_re, bt_im = btr_ref[0], bti_ref[0]
    bb_re = cf_re * bt_re - cf_im * bt_im
    bb_im = cf_re * bt_im + cf_im * bt_re
    bb_re = jnp.concatenate([bb_re] * tc, axis=0)
    bb_im = jnp.concatenate([bb_im] * tc, axis=0)
    c_re = jnp.concatenate([cr_ref[0]] * tc, axis=0)
    c_im = jnp.concatenate([ci_ref[0]] * tc, axis=0)
    step = (lax.broadcasted_iota(jnp.int32, (rows, 1), 0) >> 4).astype(F32)

    def cmul(xr, xi, yr, yi):
        return xr * yr - xi * yi, xr * yi + xi * yr

    l_re, l_im = cmul(bb_re, bb_im, *apow(-step))
    r_re, r_im = cmul(c_re, c_im, *apow(step))
    hp = lax.Precision.HIGHEST
    tm = _dot_nt(l_re, r_re, hp) - _dot_nt(l_im, r_im, hp)
    s_row = lax.broadcasted_iota(jnp.int32, (rows, rows), 0) >> 4
    t_col = lax.broadcasted_iota(jnp.int32, (rows, rows), 1) >> 4
    t_ref[0] = jnp.where(t_col >= s_row, tm, 0.0).astype(BF16)
    e_re, e_im = cmul(bb_re, bb_im, *apow((tc - 1.0) - step))
    bcr_ref[0] = e_re.astype(BF16)
    bci_ref[0] = e_im.astype(BF16)
    f_re, f_im = cmul(c_re, c_im, *apow(step + 1.0))
    ccr_ref[0] = f_re.astype(BF16)
    cci_ref[0] = (-f_im).astype(BF16)
    p_re, p_im = apow(float(tc) * one)
    a16_ref[0] = jnp.concatenate([p_re, p_im], axis=0)


def _ssm_params(a_re, a_im, log_dt, b_re, b_im, c_re, c_im):
    g, p = a_re.shape
    hg, tc = SSM_GROUP, SSM_CHUNK
    rows = tc * hg
    vec = pl.BlockSpec((1, 1, p), lambda i: (i, 0, 0))
    mat = pl.BlockSpec((1, hg, p), lambda i: (i, 0, 0))
    out_rp = pl.BlockSpec((1, rows, p), lambda i: (i, 0, 0))
    return pl.pallas_call(
        _ssm_param_kernel,
        grid=(g,),
        in_specs=[vec, vec, pl.BlockSpec((1, 1, 1), lambda i: (i, 0, 0)), mat, mat, mat, mat],
        out_specs=[pl.BlockSpec((1, rows, rows), lambda i: (i, 0, 0)), out_rp, out_rp, out_rp, out_rp,
                   pl.BlockSpec((1, 2, p), lambda i: (i, 0, 0))],
        out_shape=[jax.ShapeDtypeStruct((g, rows, rows), BF16)] +
                  [jax.ShapeDtypeStruct((g, rows, p), BF16)] * 4 +
                  [jax.ShapeDtypeStruct((g, 2, p), F32)],
        compiler_params=_cparams(("parallel",)),
        name="ssm_params",
    )(a_re.reshape(g, 1, p), a_im.reshape(g, 1, p), log_dt.reshape(g, 1, 1),
      jnp.swapaxes(b_re, 1, 2), jnp.swapaxes(b_im, 1, 2), c_re, c_im)


def _ssm_kernel(u_ref, t_ref, bcr_ref, bci_ref, ccr_ref, cci_ref, a16_ref, d_ref, y_ref,
                wre, wim, xre, xim, *, batch, n_chunks):
    u = u_ref[0]
    ub = u.astype(BF16)
    wre[...] = _dot(ub, bcr_ref[0])
    wim[...] = _dot(ub, bci_ref[0])
    a16 = a16_ref[0]
    a_r = a16[0:1]
    a_i = a16[1:2]

    def step(c, carry):
        x_r, x_i = carry
        r0 = c * batch
        if batch % 8 == 0:
            r0 = pl.multiple_of(r0, 8)
        xre[pl.ds(r0, batch), :] = x_r
        xim[pl.ds(r0, batch), :] = x_i
        w_r = wre[pl.ds(r0, batch), :]
        w_i = wim[pl.ds(r0, batch), :]
        return a_r * x_r - a_i * x_i + w_r, a_r * x_i + a_i * x_r + w_i

    zero = jnp.zeros((batch, SSM_STATE), F32)
    lax.fori_loop(0, n_chunks, step, (zero, zero))
    y = _dot(ub, t_ref[0])
    y = y + _dot_nt(xre[...].astype(BF16), ccr_ref[0]) + _dot_nt(xim[...].astype(BF16), cci_ref[0])
    y_ref[0] = y + d_ref[0] * u


def _ssm(u, params, d_skip):
    b, l, _ = u.shape
    g, hg, tc, p = N_SSM_GROUPS, SSM_GROUP, SSM_CHUNK, SSM_STATE
    nch = l // tc
    rows = nch * b
    w = tc * hg
    ug = u.reshape(b, nch, tc, g, hg).transpose(3, 1, 0, 2, 4).reshape(g, rows, w)
    d_t = jnp.broadcast_to(d_skip[:, None, :], (g, tc, hg)).reshape(g, 1, w)
    t_op, bcr, bci, ccr, cci, a16 = params
    grp = lambda r, c: pl.BlockSpec((1, r, c), lambda i: (i, 0, 0))
    y = pl.pallas_call(
        functools.partial(_ssm_kernel, batch=b, n_chunks=nch),
        grid=(g,),
        in_specs=[grp(rows, w), grp(w, w), grp(w, p), grp(w, p), grp(w, p), grp(w, p), grp(2, p),
                  grp(1, w)],
        out_specs=grp(rows, w),
        out_shape=jax.ShapeDtypeStruct((g, rows, w), F32),
        scratch_shapes=[pltpu.VMEM((rows, p), F32)] * 4,
        compiler_params=_cparams(("parallel",)),
        name="ssm_scan",
    )(ug, t_op, bcr, bci, ccr, cci, a16, d_t)
    return y.reshape(g, nch, b, tc, hg).transpose(2, 1, 3, 0, 4).reshape(b, l, g * hg)


def _mix_out_kernel(oa_ref, y_ref, x_ref, mod_ref, wglu_ref, woa_ref, wos_ref, na_ref, ns_ref,
                    lg_ref, lb_ref, o_ref):
    m = mod_ref[0]
    ya = _gelu(y_ref[0])
    o_ssm = ya * _sigmoid(_dot(ya.astype(BF16), wglu_ref[...]))
    ra = _rms_norm(oa_ref[0], na_ref[...])
    rs = _rms_norm(o_ssm, ns_ref[...])
    mix = _dot(ra.astype(BF16), woa_ref[...]) + _dot(rs.astype(BF16), wos_ref[...])
    o_ref[0] = _layer_norm(DEEPNORM_ALPHA * x_ref[0] + (1.0 + m[2:3]) * mix, lg_ref[...], lb_ref[...])


def _mix_out(o_attn, y, x, mod, w_glu, w_out_a, w_out_s, norm_attn, norm_ssm, ln_g, ln_b):
    b, l, d = x.shape
    tm = 512
    row = lambda n: pl.BlockSpec((1, tm, n), lambda i, j: (i, j, 0))
    full = lambda a: pl.BlockSpec(a.shape, lambda i, j: (0,) * a.ndim)
    consts = (w_glu, w_out_a, w_out_s, norm_attn, norm_ssm, ln_g, ln_b)
    return pl.pallas_call(
        _mix_out_kernel,
        grid=(b, l // tm),
        in_specs=[row(ATTN_WIDTH), row(SSM_WIDTH), row(d),
                  pl.BlockSpec((1, 6, d), lambda i, j: (i, 0, 0))] + [full(a) for a in consts],
        out_specs=row(d),
        out_shape=jax.ShapeDtypeStruct((b, l, d), F32),
        compiler_params=_cparams(("parallel", "parallel")),
        name="mix_out",
    )(o_attn, y, x, mod, *consts)


FF_CHUNK = 1408


def _ffn_kernel(x_ref, mod_ref, wg_ref, wu_ref, wd_ref, lg_ref, lb_ref, o_ref, h_scr, acc):
    j = pl.program_id(2)
    m = mod_ref[0]

    @pl.when(j == 0)
    def _():
        h_scr[...] = (x_ref[0] * (1.0 + m[4:5]) + m[3:4]).astype(BF16)
        acc[...] = jnp.zeros_like(acc)

    h = h_scr[...]
    gate = _dot(h, wg_ref[...])
    up = _dot(h, wu_ref[...])
    act = (gate * _sigmoid(gate) * up).astype(BF16)
    acc[...] += _dot(act, wd_ref[...])

    @pl.when(j == pl.num_programs(2) - 1)
    def _():
        o_ref[0] = _layer_norm(DEEPNORM_ALPHA * x_ref[0] + (1.0 + m[5:6]) * acc[...],
                               lg_ref[...], lb_ref[...])


def _ffn(x, mod, wg, wu, wd, ln_g, ln_b):
    b, l, d = x.shape
    tm = 512
    nf = D_FF // FF_CHUNK
    return pl.pallas_call(
        _ffn_kernel,
        grid=(b, l // tm, nf),
        in_specs=[pl.BlockSpec((1, tm, d), lambda i, r, j: (i, r, 0)),
                  pl.BlockSpec((1, 6, d), lambda i, r, j: (i, 0, 0)),
                  pl.BlockSpec((d, FF_CHUNK), lambda i, r, j: (0, j)),
                  pl.BlockSpec((d, FF_CHUNK), lambda i, r, j: (0, j)),
                  pl.BlockSpec((FF_CHUNK, d), lambda i, r, j: (j, 0)),
                  pl.BlockSpec((1, d), lambda i, r, j: (0, 0)),
                  pl.BlockSpec((1, d), lambda i, r, j: (0, 0))],
        out_specs=pl.BlockSpec((1, tm, d), lambda i, r, j: (i, r, 0)),
        out_shape=jax.ShapeDtypeStruct((b, l, d), F32),
        scratch_shapes=[pltpu.VMEM((tm, d), BF16), pltpu.VMEM((tm, d), F32)],
        compiler_params=_cparams(("parallel", "parallel", "arbitrary")),
        name="ffn",
    )(x, mod, wg, wu, wd, ln_g, ln_b)


MOE_TM = 512
META_E1, META_E2, META_W1, META_W2, META_P1, META_P2 = range(6)


def _router_kernel(x_ref, mod_ref, rt_ref, h_ref, meta_ref, cnt_ref, run):
    i = pl.program_id(0)
    tm = x_ref.shape[0]

    @pl.when(i == 0)
    def _():
        run[...] = jnp.zeros_like(run)

    m = mod_ref[0]
    h = x_ref[...] * (1.0 + m[4:5]) + m[3:4]
    h_ref[...] = h
    lane = lax.broadcasted_iota(jnp.int32, (tm, 128), 1)
    lane_f = lane.astype(F32)
    logits = jnp.dot(h, rt_ref[...], preferred_element_type=F32, precision=lax.Precision.HIGHEST)
    logits = jnp.where(lane < N_EXPERTS, logits, -jnp.inf)
    v1 = jnp.max(logits, axis=-1, keepdims=True)
    i1 = jnp.min(jnp.where(logits == v1, lane_f, 128.0), axis=-1, keepdims=True)
    rest = jnp.where(lane_f == i1, -jnp.inf, logits)
    v2 = jnp.max(rest, axis=-1, keepdims=True)
    i2 = jnp.min(jnp.where(rest == v2, lane_f, 128.0), axis=-1, keepdims=True)
    e2 = jnp.exp(v2 - v1)
    den = 1.0 + e2
    hit1 = lane_f == i1
    hit2 = lane_f == i2
    onehot = jnp.where(hit1 | hit2, 1.0, 0.0)
    earlier = (lax.broadcasted_iota(jnp.int32, (tm, tm), 0)
               > lax.broadcasted_iota(jnp.int32, (tm, tm), 1)).astype(BF16)
    slot = run[...] + _dot(earlier, onehot.astype(BF16))
    p1 = jnp.sum(jnp.where(hit1, slot, 0.0), axis=-1, keepdims=True)
    p2 = jnp.sum(jnp.where(hit2, slot, 0.0), axis=-1, keepdims=True)
    rec = jnp.zeros((tm, 128), F32)
    for col, val in ((META_E1, i1), (META_E2, i2), (META_W1, 1.0 / den), (META_W2, e2 / den),
                     (META_P1, p1), (META_P2, p2)):
        rec = jnp.where(lane == col, val, rec)
    meta_ref[...] = rec
    run[...] += jnp.sum(onehot, axis=0, keepdims=True)
    cnt_ref[...] = run[...]


def _router(xf, mod, router_pad, seq):
    t, d = xf.shape
    tm = MOE_TM
    return pl.pallas_call(
        _router_kernel,
        grid=(t // tm,),
        in_specs=[pl.BlockSpec((tm, d), lambda i: (i, 0)),
                  pl.BlockSpec((1, 6, d), lambda i: ((i * tm) // seq, 0, 0)),
                  pl.BlockSpec((d, 128), lambda i: (0, 0))],
        out_specs=[pl.BlockSpec((tm, d), lambda i: (i, 0)),
                   pl.BlockSpec((tm, 128), lambda i: (i, 0)),
                   pl.BlockSpec((1, 128), lambda i: (0, 0))],
        out_shape=[jax.ShapeDtypeStruct((t, d), F32), jax.ShapeDtypeStruct((t, 128), F32),
                   jax.ShapeDtypeStruct((1, 128), F32)],
        scratch_shapes=[pltpu.VMEM((1, 128), F32)],
        compiler_params=_cparams(("arbitrary",)),
        name="moe_router",
    )(xf, mod, router_pad)


def _dispatch_kernel(d1_ref, d2_ref, h_hbm, hs_init, hs_hbm, sem):
    del hs_init
    tm = d1_ref.shape[-1]
    base = pl.program_id(0) * tm

    def copies(t):
        src = h_hbm.at[pl.ds(base + t, 1)]
        return (pltpu.make_async_copy(src, hs_hbm.at[pl.ds(d1_ref[0, 0, t], 1)], sem),
                pltpu.make_async_copy(src, hs_hbm.at[pl.ds(d2_ref[0, 0, t], 1)], sem))

    def start(t, carry):
        for cp in copies(t):
            cp.start()
        return carry

    def wait(t, carry):
        for cp in copies(t):
            cp.wait()
        return carry

    lax.fori_loop(0, tm, start, 0)
    lax.fori_loop(0, tm, wait, 0)


def _dispatch(h, dest1, dest2, n_rows):
    t, d = h.shape
    tm = MOE_TM
    idx = pl.BlockSpec((1, 1, tm), lambda i: (i, 0, 0), memory_space=pltpu.SMEM)
    anyspec = pl.BlockSpec(memory_space=pl.ANY)
    return pl.pallas_call(
        _dispatch_kernel,
        grid=(t // tm,),
        in_specs=[idx, idx, anyspec, anyspec],
        out_specs=anyspec,
        out_shape=jax.ShapeDtypeStruct((n_rows, d), F32),
        scratch_shapes=[pltpu.SemaphoreType.DMA(())],
        input_output_aliases={3: 0},
        compiler_params=_cparams(("arbitrary",)),
        name="moe_dispatch",
    )(dest1, dest2, h, jnp.zeros((n_rows, d), F32))


def _expert_kernel(te_ref, tu_ref, hs_ref, wg_ref, wu_ref, wd_ref, o_ref, h_scr, acc):
    del te_ref
    j = pl.program_id(0)
    f = pl.program_id(1)
    last = pl.num_programs(1) - 1
    used = tu_ref[j] == 1

    @pl.when(used)
    def _():
        @pl.when(f == 0)
        def _():
            h_scr[...] = hs_ref[...].astype(BF16)
            acc[...] = jnp.zeros_like(acc)

        h = h_scr[...]
        gate = _dot(h, wg_ref[0])
        up = _dot(h, wu_ref[0])
        act = (gate * _sigmoid(gate) * up).astype(BF16)
        acc[...] += _dot(act, wd_ref[0])

        @pl.when(f == last)
        def _():
            o_ref[...] = acc[...]

    @pl.when(jnp.logical_not(used) & (f == last))
    def _():
        o_ref[...] = jnp.zeros_like(o_ref)


def _experts(hs, tile_expert, tile_used, wg, wu, wd):
    n_rows, d = hs.shape
    tm = MOE_TM
    nf = D_FF // FF_CHUNK
    grid_spec = pltpu.PrefetchScalarGridSpec(
        num_scalar_prefetch=2,
        grid=(n_rows // tm, nf),
        in_specs=[pl.BlockSpec((tm, d), lambda j, f, te, tu: (j, 0)),
                  pl.BlockSpec((1, d, FF_CHUNK), lambda j, f, te, tu: (te[j], 0, f)),
                  pl.BlockSpec((1, d, FF_CHUNK), lambda j, f, te, tu: (te[j], 0, f)),
                  pl.BlockSpec((1, FF_CHUNK, d), lambda j, f, te, tu: (te[j], f, 0))],
        out_specs=pl.BlockSpec((tm, d), lambda j, f, te, tu: (j, 0)),
        scratch_shapes=[pltpu.VMEM((tm, d), BF16), pltpu.VMEM((tm, d), F32)])
    return pl.pallas_call(
        _expert_kernel,
        grid_spec=grid_spec,
        out_shape=jax.ShapeDtypeStruct((n_rows, d), F32),
        compiler_params=_cparams(("arbitrary", "arbitrary")),
        name="moe_experts",
    )(tile_expert, tile_used, hs, wg, wu, wd)


def _combine_kernel(d1_ref, d2_ref, x_ref, mod_ref, meta_ref, lg_ref, lb_ref, ys_hbm, o_ref,
                    buf, sem):
    tm = x_ref.shape[0]

    def copies(t):
        return (pltpu.make_async_copy(ys_hbm.at[pl.ds(d1_ref[0, 0, t], 1)],
                                      buf.at[0, pl.ds(t, 1)], sem),
                pltpu.make_async_copy(ys_hbm.at[pl.ds(d2_ref[0, 0, t], 1)],
                                      buf.at[1, pl.ds(t, 1)], sem))

    def start(t, carry):
        for cp in copies(t):
            cp.start()
        return carry

    def wait(t, carry):
        for cp in copies(t):
            cp.wait()
        return carry

    lax.fori_loop(0, tm, start, 0)
    lax.fori_loop(0, tm, wait, 0)
    m = mod_ref[0]
    rec = meta_ref[...]
    f = rec[:, META_W1:META_W1 + 1] * buf[0] + rec[:, META_W2:META_W2 + 1] * buf[1]
    o_ref[...] = _layer_norm(DEEPNORM_ALPHA * x_ref[...] + (1.0 + m[5:6]) * f,
                             lg_ref[...], lb_ref[...])


def _combine(xf, mod, meta, dest1, dest2, ys, ln_g, ln_b, seq):
    t, d = xf.shape
    tm = MOE_TM
    idx = pl.BlockSpec((1, 1, tm), lambda i: (i, 0, 0), memory_space=pltpu.SMEM)
    return pl.pallas_call(
        _combine_kernel,
        grid=(t // tm,),
        in_specs=[idx, idx,
                  pl.BlockSpec((tm, d), lambda i: (i, 0)),
                  pl.BlockSpec((1, 6, d), lambda i: ((i * tm) // seq, 0, 0)),
                  pl.BlockSpec((tm, 128), lambda i: (i, 0)),
                  pl.BlockSpec((1, d), lambda i: (0, 0)),
                  pl.BlockSpec((1, d), lambda i: (0, 0)),
                  pl.BlockSpec(memory_space=pl.ANY)],
        out_specs=pl.BlockSpec((tm, d), lambda i: (i, 0)),
        out_shape=jax.ShapeDtypeStruct((t, d), F32),
        scratch_shapes=[pltpu.VMEM((2, tm, d), F32), pltpu.SemaphoreType.DMA(())],
        compiler_params=_cparams(("arbitrary",)),
        name="moe_combine",
    )(dest1, dest2, xf, mod, meta, ln_g, ln_b, ys)


def _moe(x, mod, router_pad, wg, wu, wd, ln_g, ln_b):
    b, l, d = x.shape
    t = b * l
    tm = MOE_TM
    n_tiles = 2 * t // tm + N_EXPERTS
    xf = x.reshape(t, d)
    h, meta, cnt = _router(xf, mod, router_pad, l)
    counts = cnt[0, :N_EXPERTS].astype(jnp.int32)
    tiles_e = (counts + tm - 1) // tm
    tile_end = jnp.cumsum(tiles_e)
    base = (tile_end - tiles_e) * tm
    e1 = meta[:, META_E1].astype(jnp.int32)
    e2 = meta[:, META_E2].astype(jnp.int32)
    dest1 = (base[e1] + meta[:, META_P1].astype(jnp.int32)).reshape(t // tm, 1, tm)
    dest2 = (base[e2] + meta[:, META_P2].astype(jnp.int32)).reshape(t // tm, 1, tm)
    tile_id = jnp.arange(n_tiles, dtype=jnp.int32)
    tile_expert = jnp.minimum(jnp.sum(tile_id[:, None] >= tile_end[None, :], axis=1),
                              N_EXPERTS - 1).astype(jnp.int32)
    tile_used = (tile_id < tile_end[-1]).astype(jnp.int32)
    hs = _dispatch(h, dest1, dest2, n_tiles * tm)
    ys = _experts(hs, tile_expert, tile_used, wg, wu, wd)
    out = _combine(xf, mod, meta, dest1, dest2, ys, ln_g, ln_b, l)
    return out.reshape(b, l, d)


def _reorder_w_in(w):
    q = w[:, :ATTN_WIDTH]
    kv = w[:, ATTN_WIDTH:ATTN_WIDTH + 6 * KV_WIDTH]
    g0 = ATTN_WIDTH + 6 * KV_WIDTH
    gates = w[:, g0:g0 + 24]
    u = w[:, g0 + 24:]
    pad = jnp.zeros((w.shape[0], PROJ_PAD - w.shape[1]), w.dtype)
    return jnp.concatenate([q, u, kv, gates, pad], axis=1).astype(BF16)


def kernel(x, c, w_in, cmp_pos_k, cmp_pos_v, cmp_w1_k, cmp_w2_k, cmp_w1_v, cmp_w2_v, ssm_a_re,
           ssm_a_im, ssm_log_dt, ssm_b_re, ssm_b_im, ssm_c_re, ssm_c_im, ssm_d, ssm_w_glu, norm_attn,
           norm_ssm, w_out, ada_w, ada_b, ln_g, ln_b, ffn_w_gate, ffn_w_up, ffn_w_down, moe_router,
           moe_w_gate, moe_w_up, moe_w_down):
    b, l, d = x.shape
    mod_all = _ada_mod(c, ada_w, ada_b).reshape(DEPTH, b, 6, d)
    for layer in range(DEPTH):
        mod = mod_all[layer]
        q, u, kc, vc, ks, vs, kw, vw, g = _in_proj(x, mod, _reorder_w_in(w_in[layer]))
        k_cmp, v_cmp = _compress(
            kc, vc,
            _compress_weights(cmp_pos_k[layer], cmp_w1_k[layer], cmp_w2_k[layer]),
            _compress_weights(cmp_pos_v[layer], cmp_w1_v[layer], cmp_w2_v[layer]))
        o_attn = _attention(q, g, k_cmp, v_cmp, ks, vs, kw, vw)
        params = _ssm_params(ssm_a_re[layer], ssm_a_im[layer], ssm_log_dt[layer], ssm_b_re[layer],
                             ssm_b_im[layer], ssm_c_re[layer], ssm_c_im[layer])
        y = _ssm(u, params, ssm_d[layer])
        wo = w_out[layer].astype(BF16)
        x = _mix_out(o_attn, y, x, mod, ssm_w_glu[layer].astype(BF16), wo[:ATTN_WIDTH],
                     wo[ATTN_WIDTH:], norm_attn[layer][None], norm_ssm[layer][None],
                     ln_g[layer, 0][None], ln_b[layer, 0][None])
        lg, lb = ln_g[layer, 1][None], ln_b[layer, 1][None]
        if layer % 2 == 0:
            x = _ffn(x, mod, ffn_w_gate[layer // 2].astype(BF16), ffn_w_up[layer // 2].astype(BF16),
                     ffn_w_down[layer // 2].astype(BF16), lg, lb)
        else:
            rt = jnp.pad(moe_router[layer // 2], ((0, 0), (0, 128 - N_EXPERTS)))
            x = _moe(x, mod, rt, moe_w_gate[layer // 2].astype(BF16),
                     moe_w_up[layer // 2].astype(BF16), moe_w_down[layer // 2].astype(BF16), lg, lb)
    return x
```

```python
import functools
import math

import jax
import jax.numpy as jnp
from jax import lax
from jax.experimental import pallas as pl
from jax.experimental.pallas import tpu as pltpu

F32 = jnp.float32
BF16 = jnp.bfloat16

D_MODEL = 1024
DEPTH = 4
HEAD_DIM = 64
N_Q_HEADS = 8
N_KV_HEADS = 2
Q_PER_KV = 4
ATTN_WIDTH = 512
KV_WIDTH = 128
SSM_WIDTH = 512
SSM_GROUP = 16
N_SSM_GROUPS = 32
SSM_STATE = 64
CMP_LEN = 32
CMP_STRIDE = 16
SLC_LEN = 64
N_SEL = 16
WINDOW = 512
Q_BLOCK = 128
D_FF = 2816
N_EXPERTS = 8
DEEPNORM_ALPHA = (2.0 * DEPTH) ** 0.25
LN_EPS = 1e-5
RMS_EPS = 1e-6

PROJ_PAD = 1920
SSM_CHUNK = 16
SLC_KEY_CHUNK = 512
NEG = -1e30
VMEM_LIMIT = 56 * 1024 * 1024


def _cparams(sem):
    return pltpu.CompilerParams(dimension_semantics=sem, vmem_limit_bytes=VMEM_LIMIT)


def _sigmoid(x):
    return 1.0 / (1.0 + jnp.exp(-x))


def _gelu(x):
    return 0.5 * x * (1.0 + jnp.tanh(math.sqrt(2.0 / math.pi) * (x + 0.044715 * (x * x * x))))


def _layer_norm(x, g, b):
    mu = jnp.mean(x, axis=-1, keepdims=True)
    xc = x - mu
    var = jnp.mean(xc * xc, axis=-1, keepdims=True)
    return xc * lax.rsqrt(var + LN_EPS) * g + b


def _rms_norm(x, g):
    return x * lax.rsqrt(jnp.mean(x * x, axis=-1, keepdims=True) + RMS_EPS) * g


def _dot(a, b):
    return jnp.dot(a, b, preferred_element_type=F32)


def _dot_nt(a, b, precision=None):
    return lax.dot_general(a, b, (((1,), (1,)), ((), ())), preferred_element_type=F32,
                           precision=precision)


def _ada_kernel(c_ref, w_ref, b_ref, o_ref):
    c = c_ref[...]
    ca = (c * _sigmoid(c)).astype(BF16)
    o_ref[0] = _dot(ca, w_ref[0].astype(BF16)) + b_ref[0]


def _ada_mod(c, ada_w, ada_b):
    depth, d, n = ada_w.shape
    b = c.shape[0]
    tn = 1536
    return pl.pallas_call(
        _ada_kernel,
        grid=(depth, n // tn),
        in_specs=[pl.BlockSpec((b, d), lambda l, j: (0, 0)),
                  pl.BlockSpec((1, d, tn), lambda l, j: (l, 0, j)),
                  pl.BlockSpec((1, 1, tn), lambda l, j: (l, 0, j))],
        out_specs=pl.BlockSpec((1, b, tn), lambda l, j: (l, 0, j)),
        out_shape=jax.ShapeDtypeStruct((depth, b, n), F32),
        compiler_params=_cparams(("parallel", "parallel")),
        name="ada_mod",
    )(c, ada_w, ada_b.reshape(depth, 1, n))


def _proj_kernel(x_ref, mod_ref, w_ref, q_ref, u_ref, kc_ref, vc_ref, ks_ref, vs_ref, kw_ref,
                 vw_ref, g_ref):
    m = mod_ref[0]
    h = x_ref[0] * (1.0 + m[1:2]) + m[0:1]
    r = _dot(h.astype(BF16), w_ref[...])
    q_ref[0] = (r[:, 0:512] * (HEAD_DIM ** -0.5)).astype(BF16)
    u_ref[0] = r[:, 512:1024]
    kc_ref[0] = r[:, 1024:1152]
    vc_ref[0] = r[:, 1152:1280]
    g_ref[0] = r[:, 1792:1920]
    tm = r.shape[0]
    lane = lax.broadcasted_iota(jnp.int32, (tm, 128), 1)
    key = pl.program_id(1) * tm + lax.broadcasted_iota(jnp.int32, (tm, 128), 0)
    lo_half = lane < HEAD_DIM
    block_onehot = jnp.where((key >> 6) == lane - HEAD_DIM, 1.0, 0.0)
    ones_col = jnp.where(lane == HEAD_DIM, 1.0, 0.0)
    for ref, col, const in ((ks_ref, 1280, block_onehot), (vs_ref, 1408, ones_col),
                            (kw_ref, 1536, 0.0), (vw_ref, 1664, ones_col)):
        slab = r[:, col:col + 128]
        ref[0, 0] = jnp.where(lo_half, slab, const).astype(BF16)
        ref[0, 1] = jnp.where(lo_half, pltpu.roll(slab, HEAD_DIM, 1), const).astype(BF16)


def _in_proj(x, mod, w_pad):
    b, l, d = x.shape
    tm = 512
    row = lambda n: pl.BlockSpec((1, tm, n), lambda i, j: (i, j, 0))
    shp = lambda n, dt: jax.ShapeDtypeStruct((b, l, n), dt)
    heads = pl.BlockSpec((1, N_KV_HEADS, tm, 128), lambda i, j: (i, 0, j, 0))
    hshp = jax.ShapeDtypeStruct((b, N_KV_HEADS, l, 128), BF16)
    return pl.pallas_call(
        _proj_kernel,
        grid=(b, l // tm),
        in_specs=[row(d),
                  pl.BlockSpec((1, 6, d), lambda i, j: (i, 0, 0)),
                  pl.BlockSpec((d, PROJ_PAD), lambda i, j: (0, 0))],
        out_specs=[row(512), row(512), row(128), row(128), heads, heads, heads, heads, row(128)],
        out_shape=[shp(512, BF16), shp(512, F32), shp(128, F32), shp(128, F32), hshp, hshp, hshp,
                   hshp, shp(128, F32)],
        compiler_params=_cparams(("parallel", "parallel")),
        name="in_proj",
    )(x, mod, w_pad)


def _compress_one(kr, pos_a, pos_b, w1a, w1b, w2):
    nch = kr.shape[0]
    pa = _dot((kr + pos_a).astype(BF16), w1a)
    pb = _dot((kr + pos_b).astype(BF16), w1b)
    hid = _gelu(pa + pltpu.roll(pb, nch - 1, 0))
    out = _dot(hid.astype(BF16), w2)
    rows = lax.broadcasted_iota(jnp.int32, out.shape, 0)
    return jnp.where(rows < nch - 1, out, 0.0)


def _compress_kernel(k_ref, v_ref, pak_ref, pbk_ref, w1ak_ref, w1bk_ref, w2k_ref,
                     pav_ref, pbv_ref, w1av_ref, w1bv_ref, w2v_ref, ko_ref, vo_ref):
    k = _compress_one(k_ref[0], pak_ref[...], pbk_ref[...], w1ak_ref[...], w1bk_ref[...], w2k_ref[...])
    v = _compress_one(v_ref[0], pav_ref[...], pbv_ref[...], w1av_ref[...], w1bv_ref[...], w2v_ref[...])
    lo_half = lax.broadcasted_iota(jnp.int32, k.shape, 1) < HEAD_DIM
    for ref, val in ((ko_ref, k), (vo_ref, v)):
        ref[0, 0] = jnp.where(lo_half, val, 0.0).astype(BF16)
        ref[0, 1] = jnp.where(lo_half, pltpu.roll(val, HEAD_DIM, 1), 0.0).astype(BF16)


def _compress_weights(pos, w1, w2):
    eye = jnp.eye(N_KV_HEADS, dtype=F32)
    half = CMP_STRIDE

    def big(w):
        return jnp.einsum('sdf,hg->shdgf', w, eye).reshape(half * KV_WIDTH, KV_WIDTH).astype(BF16)

    def posrow(p):
        return jnp.broadcast_to(p[:, None, :], (half, N_KV_HEADS, HEAD_DIM)).reshape(1, half * KV_WIDTH)

    w2b = jnp.einsum('fd,hg->hfgd', w2, eye).reshape(KV_WIDTH, KV_WIDTH).astype(BF16)
    return posrow(pos[:half]), posrow(pos[half:]), big(w1[:half]), big(w1[half:]), w2b


def _compress(kc, vc, wk, wv):
    b, l, _ = kc.shape
    nch = l // CMP_STRIDE
    kr = kc.reshape(b, nch, CMP_STRIDE * KV_WIDTH)
    vr = vc.reshape(b, nch, CMP_STRIDE * KV_WIDTH)
    full = lambda a: pl.BlockSpec(a.shape, lambda i: (0,) * a.ndim)
    blk = pl.BlockSpec((1, nch, CMP_STRIDE * KV_WIDTH), lambda i: (i, 0, 0))
    oblk = pl.BlockSpec((1, N_KV_HEADS, nch, KV_WIDTH), lambda i: (i, 0, 0, 0))
    return pl.pallas_call(
        _compress_kernel,
        grid=(b,),
        in_specs=[blk, blk] + [full(a) for a in wk] + [full(a) for a in wv],
        out_specs=[oblk, oblk],
        out_shape=[jax.ShapeDtypeStruct((b, N_KV_HEADS, nch, KV_WIDTH), BF16)] * 2,
        compiler_params=_cparams(("parallel",)),
        name="compress",
    )(kr, vr, *wk, *wv)


def _attn_kernel(q_ref, g_ref, kc_ref, vc_ref, ks_ref, vs_ref, kw_ref, vw_ref, o_ref, *, seq):
    nch = seq // CMP_STRIDE
    n_cmp = nch - 1
    n_slc = seq // SLC_LEN
    n_sel = min(N_SEL, n_slc)
    qb, gq, hd = Q_BLOCK, Q_PER_KV, HEAD_DIM
    nb = HEAD_DIM
    i = pl.program_id(1)
    qs = i * qb
    t_col = qs + lax.broadcasted_iota(jnp.int32, (qb, 1), 0)
    t_row = qs + lax.broadcasted_iota(jnp.int32, (1, qb), 1)
    lo_half = lax.broadcasted_iota(jnp.int32, (1, 128), 1) < hd

    gates = _sigmoid(g_ref[0])

    q_heads = []
    for m in range(N_Q_HEADS // 2):
        slab = q_ref[0, :, 128 * m:128 * (m + 1)]
        q_heads.append(slab)
        q_heads.append(pltpu.roll(slab.astype(F32), hd, 1).astype(BF16))

    n_idx = lax.broadcasted_iota(jnp.int32, (1, nch), 1)
    cmp_valid = ((n_idx * CMP_STRIDE + (CMP_LEN - 1)) <= t_col) & (n_idx < n_cmp)
    cmp_bias = jnp.where(cmp_valid, 0.0, NEG)
    cmp_keep = cmp_valid.astype(F32)
    oj = lax.broadcasted_iota(jnp.int32, (nb, nch), 0) * SLC_LEN
    on = lax.broadcasted_iota(jnp.int32, (nb, nch), 1) * CMP_STRIDE
    overlap_t = ((on < oj + SLC_LEN) & (on + CMP_LEN > oj)).astype(F32)

    j_col = lax.broadcasted_iota(jnp.int32, (nb, 1), 0)
    cur = t_row >> 6
    causal_t = (j_col * SLC_LEN) <= t_row
    forced_t = (j_col == 0) | (j_col == cur) | (j_col == cur - 1)
    sub = lax.broadcasted_iota(jnp.int32, (8, 1), 0)

    heads = range(N_Q_HEADS)
    kv_heads = range(N_KV_HEADS)
    kcb = [kc_ref[0, h] for h in kv_heads]
    vcb = [vc_ref[0, h] for h in kv_heads]
    s_c = [_dot_nt(q_heads[hq], kcb[hq // gq]) + cmp_bias for hq in heads]
    e_c = [jnp.exp(s_c[hq] - jnp.max(s_c[hq], axis=-1, keepdims=True)) * cmp_keep for hq in heads]
    den = [jnp.sum(e_c[hq], axis=-1, keepdims=True) for hq in heads]
    p_c = [e_c[hq] * (1.0 / jnp.where(den[hq] > 0, den[hq], 1.0)) for hq in heads]
    o_cmp = [_dot(p_c[hq].astype(BF16), vcb[hq // gq]) for hq in heads]
    p_sum = [p_c[gq * h] + p_c[gq * h + 1] + p_c[gq * h + 2] + p_c[gq * h + 3] for h in kv_heads]
    imp_t = [_dot_nt(overlap_t, p_sum[h], lax.Precision.HIGHEST) for h in kv_heads]

    score_t = [jnp.where(causal_t, jnp.where(forced_t, jnp.inf, imp_t[h]), -jnp.inf)
               for h in kv_heads]
    n_vb = nb // 8
    blocks = [[score_t[h][8 * v:8 * (v + 1)] for v in range(n_vb)] for h in kv_heads]
    ranks = [[jnp.zeros((8, qb), F32) for _ in range(n_vb)] for h in kv_heads]
    for b in range(n_slc):
        for h in kv_heads:
            row = jnp.broadcast_to(score_t[h][b:b + 1], (8, qb))
            for v in range(n_vb):
                if 8 * v > b:
                    beats = row >= blocks[h][v]
                elif 8 * v + 7 < b:
                    beats = row > blocks[h][v]
                else:
                    beats = (row > blocks[h][v]) | ((row == blocks[h][v]) & (sub + 8 * v > b))
                ranks[h][v] = ranks[h][v] + jnp.where(beats, 1.0, 0.0)
    q_sel = []
    for h in kv_heads:
        rank_t = jnp.concatenate(ranks[h], axis=0)
        sel_neg_t = jnp.where((rank_t < n_sel) & causal_t, 0.0, NEG)
        sel_neg = jnp.concatenate([sel_neg_t, sel_neg_t], axis=0).T.astype(BF16)
        for g in range(gq):
            q_sel.append(jnp.where(lo_half, q_heads[gq * h + g], sel_neg))

    n_tot = (qs + qb + SLC_KEY_CHUNK - 1) // SLC_KEY_CHUNK
    last_k0 = (n_tot - 1) * SLC_KEY_CHUNK
    kpos = last_k0 + lax.broadcasted_iota(jnp.int32, (1, SLC_KEY_CHUNK), 1)
    diag_bias = jnp.where(kpos <= t_col, 0.0, NEG)

    def slc_chunk(k0, carry, bias):
        heads = range(N_Q_HEADS)
        kblk = [ks_ref[0, h, pl.ds(k0, SLC_KEY_CHUNK), :] for h in range(N_KV_HEADS)]
        vblk = [vs_ref[0, h, pl.ds(k0, SLC_KEY_CHUNK), :] for h in range(N_KV_HEADS)]
        s = [_dot_nt(q_sel[hq], kblk[hq // gq]) for hq in heads]
        if bias is not None:
            s = [x + bias for x in s]
        m_new = [jnp.maximum(carry[hq][0], jnp.max(s[hq], axis=-1, keepdims=True)) for hq in heads]
        p = [jnp.exp(s[hq] - m_new[hq]).astype(BF16) for hq in heads]
        pv = [_dot(p[hq], vblk[hq // gq]) for hq in heads]
        return tuple((m_new[hq], jnp.exp(carry[hq][0] - m_new[hq]) * carry[hq][1] + pv[hq])
                     for hq in heads)

    init = tuple((jnp.full((qb, 1), NEG, F32), jnp.zeros((qb, 128), F32))
                 for _ in range(N_Q_HEADS))
    carry = lax.fori_loop(
        0, n_tot - 1,
        lambda c, cr: slc_chunk(pl.multiple_of(c * SLC_KEY_CHUNK, SLC_KEY_CHUNK), cr, None), init)
    carry = slc_chunk(pl.multiple_of(last_k0, SLC_KEY_CHUNK), carry, diag_bias)
    o_slc = [acc for _, acc in carry]

    win_start = pl.multiple_of(jnp.maximum(qs - WINDOW, 0), qb)
    wpos = win_start + lax.broadcasted_iota(jnp.int32, (1, WINDOW + qb), 1)
    dist = t_col - wpos
    win_bias = jnp.where((dist >= 0) & (dist < WINDOW), 0.0, NEG)

    kwb = [kw_ref[0, h, pl.ds(win_start, WINDOW + qb), :] for h in kv_heads]
    vwb = [vw_ref[0, h, pl.ds(win_start, WINDOW + qb), :] for h in kv_heads]
    s_w = [_dot_nt(q_heads[hq], kwb[hq // gq]) + win_bias for hq in heads]
    e_w = [jnp.exp(s_w[hq] - jnp.max(s_w[hq], axis=-1, keepdims=True)).astype(BF16)
           for hq in heads]
    o_win = [_dot(e_w[hq], vwb[hq // gq]) for hq in heads]

    res = []
    for hq in range(N_Q_HEADS):
        c0 = 3 * hq
        w_slc = gates[:, c0 + 1:c0 + 2] / o_slc[hq][:, hd:hd + 1]
        w_win = gates[:, c0 + 2:c0 + 3] / o_win[hq][:, hd:hd + 1]
        res.append(gates[:, c0:c0 + 1] * o_cmp[hq] + w_slc * o_slc[hq] + w_win * o_win[hq])
    for m in range(N_Q_HEADS // 2):
        o_ref[0, :, 128 * m:128 * (m + 1)] = jnp.where(
            lo_half, res[2 * m], pltpu.roll(res[2 * m + 1], hd, 1))


def _attention(q, g, k_cmp, v_cmp, ks, vs, kw, vw):
    b, l, _ = q.shape
    nch = l // CMP_STRIDE
    assert l // SLC_LEN <= HEAD_DIM and l >= WINDOW + Q_BLOCK and l % SLC_KEY_CHUNK == 0
    tile = lambda n: pl.BlockSpec((1, Q_BLOCK, n), lambda i, j: (i, j, 0))
    whole = lambda r: pl.BlockSpec((1, N_KV_HEADS, r, KV_WIDTH), lambda i, j: (i, 0, 0, 0))
    return pl.pallas_call(
        functools.partial(_attn_kernel, seq=l),
        grid=(b, l // Q_BLOCK),
        in_specs=[tile(ATTN_WIDTH), tile(128), whole(nch), whole(nch), whole(l), whole(l),
                  whole(l), whole(l)],
        out_specs=tile(ATTN_WIDTH),
        out_shape=jax.ShapeDtypeStruct((b, l, ATTN_WIDTH), F32),
        compiler_params=_cparams(("parallel", "arbitrary")),
        name="nsa_attention",
    )(q, g, k_cmp, v_cmp, ks, vs, kw, vw)


def _ssm_param_kernel(are_ref, aim_ref, ldt_ref, btr_ref, bti_ref, cr_ref, ci_ref,
                      t_ref, bcr_ref, bci_ref, ccr_ref, cci_ref, a16_ref):
    tc, hg, p = SSM_CHUNK, SSM_GROUP, SSM_STATE
    rows = tc * hg
    a_re = are_ref[0]
    a_im = aim_ref[0]
    dt = jnp.exp(ldt_ref[0])
    lam_re = a_re * dt
    lam_im = a_im * dt

    def apow(k):
        mag = jnp.exp(k * lam_re)
        return mag * jnp.cos(k * lam_im), mag * jnp.sin(k * lam_im)

    one = jnp.ones((1, 1), F32)
    ab_re, ab_im = apow(one)
    nr, ni = ab_re - 1.0, ab_im
    inv = 1.0 / (a_re * a_re + a_im * a_im)
    cf_re = (nr * a_re + ni * a_im) * inv
    cf_im = (ni * a_re - nr * a_im) * inv
    bt_re, bt_im = btr_ref[0], bti_ref[0]
    bb_re = cf_re * bt_re - cf_im * bt_im
    bb_im = cf_re * bt_im + cf_im * bt_re
    bb_re = jnp.concatenate([bb_re] * tc, axis=0)
    bb_im = jnp.concatenate([bb_im] * tc, axis=0)
    c_re = jnp.concatenate([cr_ref[0]] * tc, axis=0)
    c_im = jnp.concatenate([ci_ref[0]] * tc, axis=0)
    step = (lax.broadcasted_iota(jnp.int32, (rows, 1), 0) >> 4).astype(F32)

    def cmul(xr, xi, yr, yi):
        return xr * yr - xi * yi, xr * yi + xi * yr

    l_re, l_im = cmul(bb_re, bb_im, *apow(-step))
    r_re, r_im = cmul(c_re, c_im, *apow(step))
    hp = lax.Precision.HIGHEST
    tm = _dot_nt(l_re, r_re, hp) - _dot_nt(l_im, r_im, hp)
    s_row = lax.broadcasted_iota(jnp.int32, (rows, rows), 0) >> 4
    t_col = lax.broadcasted_iota(jnp.int32, (rows, rows), 1) >> 4
    t_ref[0] = jnp.where(t_col >= s_row, tm, 0.0).astype(BF16)
    e_re, e_im = cmul(bb_re, bb_im, *apow((tc - 1.0) - step))
    bcr_ref[0] = e_re.astype(BF16)
    bci_ref[0] = e_im.astype(BF16)
    f_re, f_im = cmul(c_re, c_im, *apow(step + 1.0))
    ccr_ref[0] = f_re.astype(BF16)
    cci_ref[0] = (-f_im).astype(BF16)
    p_re, p_im = apow(float(tc) * one)
    a16_ref[0] = jnp.concatenate([p_re, p_im], axis=0)


def _ssm_params(a_re, a_im, log_dt, b_re, b_im, c_re, c_im):
    g, p = a_re.shape
    hg, tc = SSM_GROUP, SSM_CHUNK
    rows = tc * hg
    vec = pl.BlockSpec((1, 1, p), lambda i: (i, 0, 0))
    mat = pl.BlockSpec((1, hg, p), lambda i: (i, 0, 0))
    out_rp = pl.BlockSpec((1, rows, p), lambda i: (i, 0, 0))
    return pl.pallas_call(
        _ssm_param_kernel,
        grid=(g,),
        in_specs=[vec, vec, pl.BlockSpec((1, 1, 1), lambda i: (i, 0, 0)), mat, mat, mat, mat],
        out_specs=[pl.BlockSpec((1, rows, rows), lambda i: (i, 0, 0)), out_rp, out_rp, out_rp, out_rp,
                   pl.BlockSpec((1, 2, p), lambda i: (i, 0, 0))],
        out_shape=[jax.ShapeDtypeStruct((g, rows, rows), BF16)] +
                  [jax.ShapeDtypeStruct((g, rows, p), BF16)] * 4 +
                  [jax.ShapeDtypeStruct((g, 2, p), F32)],
        compiler_params=_cparams(("parallel",)),
        name="ssm_params",
    )(a_re.reshape(g, 1, p), a_im.reshape(g, 1, p), log_dt.reshape(g, 1, 1),
      jnp.swapaxes(b_re, 1, 2), jnp.swapaxes(b_im, 1, 2), c_re, c_im)


def _ssm_kernel(u_ref, t_ref, bcr_ref, bci_ref, ccr_ref, cci_ref, a16_ref, d_ref, y_ref,
                wre, wim, xre, xim, *, batch, n_chunks):
    u = u_ref[0]
    ub = u.astype(BF16)
    wre[...] = _dot(ub, bcr_ref[0])
    wim[...] = _dot(ub, bci_ref[0])
    a16 = a16_ref[0]
    a_r = a16[0:1]
    a_i = a16[1:2]

    def step(c, carry):
        x_r, x_i = carry
        r0 = c * batch
        if batch % 8 == 0:
            r0 = pl.multiple_of(r0, 8)
        xre[pl.ds(r0, batch), :] = x_r
        xim[pl.ds(r0, batch), :] = x_i
        w_r = wre[pl.ds(r0, batch), :]
        w_i = wim[pl.ds(r0, batch), :]
        return a_r * x_r - a_i * x_i + w_r, a_r * x_i + a_i * x_r + w_i

    zero = jnp.zeros((batch, SSM_STATE), F32)
    lax.fori_loop(0, n_chunks, step, (zero, zero))
    y = _dot(ub, t_ref[0])
    y = y + _dot_nt(xre[...].astype(BF16), ccr_ref[0]) + _dot_nt(xim[...].astype(BF16), cci_ref[0])
    y_ref[0] = y + d_ref[0] * u


def _ssm(u, params, d_skip):
    b, l, _ = u.shape
    g, hg, tc, p = N_SSM_GROUPS, SSM_GROUP, SSM_CHUNK, SSM_STATE
    nch = l // tc
    rows = nch * b
    w = tc * hg
    ug = u.reshape(b, nch, tc, g, hg).transpose(3, 1, 0, 2, 4).reshape(g, rows, w)
    d_t = jnp.broadcast_to(d_skip[:, None, :], (g, tc, hg)).reshape(g, 1, w)
    t_op, bcr, bci, ccr, cci, a16 = params
    grp = lambda r, c: pl.BlockSpec((1, r, c), lambda i: (i, 0, 0))
    y = pl.pallas_call(
        functools.partial(_ssm_kernel, batch=b, n_chunks=nch),
        grid=(g,),
        in_specs=[grp(rows, w), grp(w, w), grp(w, p), grp(w, p), grp(w, p), grp(w, p), grp(2, p),
                  grp(1, w)],
        out_specs=grp(rows, w),
        out_shape=jax.ShapeDtypeStruct((g, rows, w), F32),
        scratch_shapes=[pltpu.VMEM((rows, p), F32)] * 4,
        compiler_params=_cparams(("parallel",)),
        name="ssm_scan",
    )(ug, t_op, bcr, bci, ccr, cci, a16, d_t)
    return y.reshape(g, nch, b, tc, hg).transpose(2, 1, 3, 0, 4).reshape(b, l, g * hg)


def _mix_out_kernel(oa_ref, y_ref, x_ref, mod_ref, wglu_ref, woa_ref, wos_ref, na_ref, ns_ref,
                    lg_ref, lb_ref, o_ref):
    m = mod_ref[0]
    ya = _gelu(y_ref[0])
    o_ssm = ya * _sigmoid(_dot(ya.astype(BF16), wglu_ref[...]))
    ra = _rms_norm(oa_ref[0], na_ref[...])
    rs = _rms_norm(o_ssm, ns_ref[...])
    mix = _dot(ra.astype(BF16), woa_ref[...]) + _dot(rs.astype(BF16), wos_ref[...])
    o_ref[0] = _layer_norm(DEEPNORM_ALPHA * x_ref[0] + (1.0 + m[2:3]) * mix, lg_ref[...], lb_ref[...])


def _mix_out(o_attn, y, x, mod, w_glu, w_out_a, w_out_s, norm_attn, norm_ssm, ln_g, ln_b):
    b, l, d = x.shape
    tm = 512
    row = lambda n: pl.BlockSpec((1, tm, n), lambda i, j: (i, j, 0))
    full = lambda a: pl.BlockSpec(a.shape, lambda i, j: (0,) * a.ndim)
    consts = (w_glu, w_out_a, w_out_s, norm_attn, norm_ssm, ln_g, ln_b)
    return pl.pallas_call(
        _mix_out_kernel,
        grid=(b, l // tm),
        in_specs=[row(ATTN_WIDTH), row(SSM_WIDTH), row(d),
                  pl.BlockSpec((1, 6, d), lambda i, j: (i, 0, 0))] + [full(a) for a in consts],
        out_specs=row(d),
        out_shape=jax.ShapeDtypeStruct((b, l, d), F32),
        compiler_params=_cparams(("parallel", "parallel")),
        name="mix_out",
    )(o_attn, y, x, mod, *consts)


FF_CHUNK = 1408


def _ffn_kernel(x_ref, mod_ref, wg_ref, wu_ref, wd_ref, lg_ref, lb_ref, o_ref, h_scr, acc):
    j = pl.program_id(2)
    m = mod_ref[0]

    @pl.when(j == 0)
    def _():
        h_scr[...] = (x_ref[0] * (1.0 + m[4:5]) + m[3:4]).astype(BF16)
        acc[...] = jnp.zeros_like(acc)

    h = h_scr[...]
    gate = _dot(h, wg_ref[...])
    up = _dot(h, wu_ref[...])
    act = (gate * _sigmoid(gate) * up).astype(BF16)
    acc[...] += _dot(act, wd_ref[...])

    @pl.when(j == pl.num_programs(2) - 1)
    def _():
        o_ref[0] = _layer_norm(DEEPNORM_ALPHA * x_ref[0] + (1.0 + m[5:6]) * acc[...],
                               lg_ref[...], lb_ref[...])


def _ffn(x, mod, wg, wu, wd, ln_g, ln_b):
    b, l, d = x.shape
    tm = 512
    nf = D_FF // FF_CHUNK
    return pl.pallas_call(
        _ffn_kernel,
        grid=(b, l // tm, nf),
        in_specs=[pl.BlockSpec((1, tm, d), lambda i, r, j: (i, r, 0)),
                  pl.BlockSpec((1, 6, d), lambda i, r, j: (i, 0, 0)),
                  pl.BlockSpec((d, FF_CHUNK), lambda i, r, j: (0, j)),
                  pl.BlockSpec((d, FF_CHUNK), lambda i, r, j: (0, j)),
                  pl.BlockSpec((FF_CHUNK, d), lambda i, r, j: (j, 0)),
                  pl.BlockSpec((1, d), lambda i, r, j: (0, 0)),
                  pl.BlockSpec((1, d), lambda i, r, j: (0, 0))],
        out_specs=pl.BlockSpec((1, tm, d), lambda i, r, j: (i, r, 0)),
        out_shape=jax.ShapeDtypeStruct((b, l, d), F32),
        scratch_shapes=[pltpu.VMEM((tm, d), BF16), pltpu.VMEM((tm, d), F32)],
        compiler_params=_cparams(("parallel", "parallel", "arbitrary")),
        name="ffn",
    )(x, mod, wg, wu, wd, ln_g, ln_b)


MOE_TM = 512
META_E1, META_E2, META_W1, META_W2, META_P1, META_P2 = range(6)
REC_SUB = 8


def _store_records(ref, val):
    rows = val.shape[0]
    for s in range(REC_SUB):
        ref[pl.ds(s, rows, stride=REC_SUB), :] = val[:, 128 * s:128 * (s + 1)]


def _load_records(ref):
    rows = ref.shape[0] // REC_SUB
    return jnp.concatenate([ref[pl.ds(s, rows, stride=REC_SUB), :] for s in range(REC_SUB)], axis=1)


def _router_kernel(x_ref, mod_ref, rt_ref, h_ref, meta_ref, cnt_ref, run):
    i = pl.program_id(0)
    tm = x_ref.shape[0]

    @pl.when(i == 0)
    def _():
        run[...] = jnp.zeros_like(run)

    m = mod_ref[0]
    h = x_ref[...] * (1.0 + m[4:5]) + m[3:4]
    _store_records(h_ref, h)
    lane = lax.broadcasted_iota(jnp.int32, (tm, 128), 1)
    lane_f = lane.astype(F32)
    logits = jnp.dot(h, rt_ref[...], preferred_element_type=F32, precision=lax.Precision.HIGHEST)
    logits = jnp.where(lane < N_EXPERTS, logits, -jnp.inf)
    v1 = jnp.max(logits, axis=-1, keepdims=True)
    i1 = jnp.min(jnp.where(logits == v1, lane_f, 128.0), axis=-1, keepdims=True)
    rest = jnp.where(lane_f == i1, -jnp.inf, logits)
    v2 = jnp.max(rest, axis=-1, keepdims=True)
    i2 = jnp.min(jnp.where(rest == v2, lane_f, 128.0), axis=-1, keepdims=True)
    e2 = jnp.exp(v2 - v1)
    den = 1.0 + e2
    hit1 = lane_f == i1
    hit2 = lane_f == i2
    onehot = jnp.where(hit1 | hit2, 1.0, 0.0)
    earlier = (lax.broadcasted_iota(jnp.int32, (tm, tm), 0)
               > lax.broadcasted_iota(jnp.int32, (tm, tm), 1)).astype(BF16)
    slot = run[...] + _dot(earlier, onehot.astype(BF16))
    p1 = jnp.sum(jnp.where(hit1, slot, 0.0), axis=-1, keepdims=True)
    p2 = jnp.sum(jnp.where(hit2, slot, 0.0), axis=-1, keepdims=True)
    rec = jnp.zeros((tm, 128), F32)
    for col, val in ((META_E1, i1), (META_E2, i2), (META_W1, 1.0 / den), (META_W2, e2 / den),
                     (META_P1, p1), (META_P2, p2)):
        rec = jnp.where(lane == col, val, rec)
    meta_ref[...] = rec
    run[...] += jnp.sum(onehot, axis=0, keepdims=True)
    cnt_ref[...] = run[...]


def _router(xf, mod, router_pad, seq):
    t, d = xf.shape
    tm = MOE_TM
    return pl.pallas_call(
        _router_kernel,
        grid=(t // tm,),
        in_specs=[pl.BlockSpec((tm, d), lambda i: (i, 0)),
                  pl.BlockSpec((1, 6, d), lambda i: ((i * tm) // seq, 0, 0)),
                  pl.BlockSpec((d, 128), lambda i: (0, 0))],
        out_specs=[pl.BlockSpec((tm * REC_SUB, 128), lambda i: (i, 0)),
                   pl.BlockSpec((tm, 128), lambda i: (i, 0)),
                   pl.BlockSpec((1, 128), lambda i: (0, 0))],
        out_shape=[jax.ShapeDtypeStruct((t * REC_SUB, 128), F32),
                   jax.ShapeDtypeStruct((t, 128), F32), jax.ShapeDtypeStruct((1, 128), F32)],
        scratch_shapes=[pltpu.VMEM((1, 128), F32)],
        compiler_params=_cparams(("arbitrary",)),
        name="moe_router",
    )(xf, mod, router_pad)


def _dispatch_kernel(d1_ref, d2_ref, h_hbm, hs_init, hs_hbm, sem):
    del hs_init
    tm = d1_ref.shape[-1]
    base = pl.program_id(0) * tm

    def record(ref, r):
        return ref.at[pl.ds(pl.multiple_of(r * REC_SUB, REC_SUB), REC_SUB)]

    def copies(t):
        src = record(h_hbm, base + t)
        return (pltpu.make_async_copy(src, record(hs_hbm, d1_ref[0, 0, t]), sem),
                pltpu.make_async_copy(src, record(hs_hbm, d2_ref[0, 0, t]), sem))

    def start(t, carry):
        for cp in copies(t):
            cp.start()
        return carry

    def wait(t, carry):
        for cp in copies(t):
            cp.wait()
        return carry

    lax.fori_loop(0, tm, start, 0)
    lax.fori_loop(0, tm, wait, 0)


def _dispatch(h, dest1, dest2, n_rows):
    t = h.shape[0] // REC_SUB
    tm = MOE_TM
    idx = pl.BlockSpec((1, 1, tm), lambda i: (i, 0, 0), memory_space=pltpu.SMEM)
    anyspec = pl.BlockSpec(memory_space=pl.ANY)
    return pl.pallas_call(
        _dispatch_kernel,
        grid=(t // tm,),
        in_specs=[idx, idx, anyspec, anyspec],
        out_specs=anyspec,
        out_shape=jax.ShapeDtypeStruct((n_rows * REC_SUB, 128), F32),
        scratch_shapes=[pltpu.SemaphoreType.DMA(())],
        input_output_aliases={3: 0},
        compiler_params=_cparams(("arbitrary",)),
        name="moe_dispatch",
    )(dest1, dest2, h, jnp.zeros((n_rows * REC_SUB, 128), F32))


def _expert_kernel(te_ref, tu_ref, hs_ref, wg_ref, wu_ref, wd_ref, o_ref, h_scr, acc):
    del te_ref
    j = pl.program_id(0)
    f = pl.program_id(1)
    last = pl.num_programs(1) - 1
    used = tu_ref[j] == 1

    @pl.when(used)
    def _():
        @pl.when(f == 0)
        def _():
            h_scr[...] = _load_records(hs_ref).astype(BF16)
            acc[...] = jnp.zeros_like(acc)

        h = h_scr[...]
        gate = _dot(h, wg_ref[0])
        up = _dot(h, wu_ref[0])
        act = (gate * _sigmoid(gate) * up).astype(BF16)
        acc[...] += _dot(act, wd_ref[0])

        @pl.when(f == last)
        def _():
            _store_records(o_ref, acc[...])

    @pl.when(jnp.logical_not(used) & (f == last))
    def _():
        o_ref[...] = jnp.zeros_like(o_ref)


def _experts(hs, tile_expert, tile_used, wg, wu, wd):
    n_rows = hs.shape[0] // REC_SUB
    d = D_MODEL
    tm = MOE_TM
    nf = D_FF // FF_CHUNK
    rec = pl.BlockSpec((tm * REC_SUB, 128), lambda j, f, te, tu: (j, 0))
    grid_spec = pltpu.PrefetchScalarGridSpec(
        num_scalar_prefetch=2,
        grid=(n_rows // tm, nf),
        in_specs=[rec,
                  pl.BlockSpec((1, d, FF_CHUNK), lambda j, f, te, tu: (te[j], 0, f)),
                  pl.BlockSpec((1, d, FF_CHUNK), lambda j, f, te, tu: (te[j], 0, f)),
                  pl.BlockSpec((1, FF_CHUNK, d), lambda j, f, te, tu: (te[j], f, 0))],
        out_specs=rec,
        scratch_shapes=[pltpu.VMEM((tm, d), BF16), pltpu.VMEM((tm, d), F32)])
    return pl.pallas_call(
        _expert_kernel,
        grid_spec=grid_spec,
        out_shape=jax.ShapeDtypeStruct((n_rows * REC_SUB, 128), F32),
        compiler_params=_cparams(("arbitrary", "arbitrary")),
        name="moe_experts",
    )(tile_expert, tile_used, hs, wg, wu, wd)


def _combine_kernel(d1_ref, d2_ref, x_ref, mod_ref, meta_ref, lg_ref, lb_ref, ys_hbm, o_ref,
                    buf, sem):
    tm = x_ref.shape[0]

    def record(ref, r):
        return ref.at[pl.ds(pl.multiple_of(r * REC_SUB, REC_SUB), REC_SUB)]

    def copies(t):
        return (pltpu.make_async_copy(record(ys_hbm, d1_ref[0, 0, t]), record(buf.at[0], t), sem),
                pltpu.make_async_copy(record(ys_hbm, d2_ref[0, 0, t]), record(buf.at[1], t), sem))

    def start(t, carry):
        for cp in copies(t):
            cp.start()
        return carry

    def wait(t, carry):
        for cp in copies(t):
            cp.wait()
        return carry

    lax.fori_loop(0, tm, start, 0)
    lax.fori_loop(0, tm, wait, 0)
    m = mod_ref[0]
    rec = meta_ref[...]
    f = (rec[:, META_W1:META_W1 + 1] * _load_records(buf.at[0])
         + rec[:, META_W2:META_W2 + 1] * _load_records(buf.at[1]))
    o_ref[...] = _layer_norm(DEEPNORM_ALPHA * x_ref[...] + (1.0 + m[5:6]) * f,
                             lg_ref[...], lb_ref[...])


def _combine(xf, mod, meta, dest1, dest2, ys, ln_g, ln_b, seq):
    t, d = xf.shape
    tm = MOE_TM
    idx = pl.BlockSpec((1, 1, tm), lambda i: (i, 0, 0), memory_space=pltpu.SMEM)
    return pl.pallas_call(
        _combine_kernel,
        grid=(t // tm,),
        in_specs=[idx, idx,
                  pl.BlockSpec((tm, d), lambda i: (i, 0)),
                  pl.BlockSpec((1, 6, d), lambda i: ((i * tm) // seq, 0, 0)),
                  pl.BlockSpec((tm, 128), lambda i: (i, 0)),
                  pl.BlockSpec((1, d), lambda i: (0, 0)),
                  pl.BlockSpec((1, d), lambda i: (0, 0)),
                  pl.BlockSpec(memory_space=pl.ANY)],
        out_specs=pl.BlockSpec((tm, d), lambda i: (i, 0)),
        out_shape=jax.ShapeDtypeStruct((t, d), F32),
        scratch_shapes=[pltpu.VMEM((2, tm * REC_SUB, 128), F32), pltpu.SemaphoreType.DMA(())],
        compiler_params=_cparams(("arbitrary",)),
        name="moe_combine",
    )(dest1, dest2, xf, mod, meta, ln_g, ln_b, ys)


def _moe(x, mod, router_pad, wg, wu, wd, ln_g, ln_b):
    b, l, d = x.shape
    t = b * l
    tm = MOE_TM
    n_tiles = 2 * t // tm + N_EXPERTS
    xf = x.reshape(t, d)
    h, meta, cnt = _router(xf, mod, router_pad, l)
    counts = cnt[0, :N_EXPERTS].astype(jnp.int32)
    tiles_e = (counts + tm - 1) // tm
    tile_end = jnp.cumsum(tiles_e)
    base = (tile_end - tiles_e) * tm
    e1 = meta[:, META_E1].astype(jnp.int32)
    e2 = meta[:, META_E2].astype(jnp.int32)
    dest1 = (base[e1] + meta[:, META_P1].astype(jnp.int32)).reshape(t // tm, 1, tm)
    dest2 = (base[e2] + meta[:, META_P2].astype(jnp.int32)).reshape(t // tm, 1, tm)
    tile_id = jnp.arange(n_tiles, dtype=jnp.int32)
    tile_expert = jnp.minimum(jnp.sum(tile_id[:, None] >= tile_end[None, :], axis=1),
                              N_EXPERTS - 1).astype(jnp.int32)
    tile_used = (tile_id < tile_end[-1]).astype(jnp.int32)
    hs = _dispatch(h, dest1, dest2, n_tiles * tm)
    ys = _experts(hs, tile_expert, tile_used, wg, wu, wd)
    out = _combine(xf, mod, meta, dest1, dest2, ys, ln_g, ln_b, l)
    return out.reshape(b, l, d)


def _reorder_w_in(w):
    q = w[:, :ATTN_WIDTH]
    kv = w[:, ATTN_WIDTH:ATTN_WIDTH + 6 * KV_WIDTH]
    g0 = ATTN_WIDTH + 6 * KV_WIDTH
    gates = w[:, g0:g0 + 24]
    u = w[:, g0 + 24:]
    pad = jnp.zeros((w.shape[0], PROJ_PAD - w.shape[1]), w.dtype)
    return jnp.concatenate([q, u, kv, gates, pad], axis=1).astype(BF16)


def kernel(x, c, w_in, cmp_pos_k, cmp_pos_v, cmp_w1_k, cmp_w2_k, cmp_w1_v, cmp_w2_v, ssm_a_re,
           ssm_a_im, ssm_log_dt, ssm_b_re, ssm_b_im, ssm_c_re, ssm_c_im, ssm_d, ssm_w_glu, norm_attn,
           norm_ssm, w_out, ada_w, ada_b, ln_g, ln_b, ffn_w_gate, ffn_w_up, ffn_w_down, moe_router,
           moe_w_gate, moe_w_up, moe_w_down):
    b, l, d = x.shape
    mod_all = _ada_mod(c, ada_w, ada_b).reshape(DEPTH, b, 6, d)
    for layer in range(DEPTH):
        mod = mod_all[layer]
        q, u, kc, vc, ks, vs, kw, vw, g = _in_proj(x, mod, _reorder_w_in(w_in[layer]))
        k_cmp, v_cmp = _compress(
            kc, vc,
            _compress_weights(cmp_pos_k[layer], cmp_w1_k[layer], cmp_w2_k[layer]),
            _compress_weights(cmp_pos_v[layer], cmp_w1_v[layer], cmp_w2_v[layer]))
        o_attn = _attention(q, g, k_cmp, v_cmp, ks, vs, kw, vw)
        params = _ssm_params(ssm_a_re[layer], ssm_a_im[layer], ssm_log_dt[layer], ssm_b_re[layer],
                             ssm_b_im[layer], ssm_c_re[layer], ssm_c_im[layer])
        y = _ssm(u, params, ssm_d[layer])
        wo = w_out[layer].astype(BF16)
        x = _mix_out(o_attn, y, x, mod, ssm_w_glu[layer].astype(BF16), wo[:ATTN_WIDTH],
                     wo[ATTN_WIDTH:], norm_attn[layer][None], norm_ssm[layer][None],
                     ln_g[layer, 0][None], ln_b[layer, 0][None])
        lg, lb = ln_g[layer, 1][None], ln_b[layer, 1][None]
        if layer % 2 == 0:
            x = _ffn(x, mod, ffn_w_gate[layer // 2].astype(BF16), ffn_w_up[layer // 2].astype(BF16),
                     ffn_w_down[layer // 2].astype(BF16), lg, lb)
        else:
            rt = jnp.pad(moe_router[layer // 2], ((0, 0), (0, 128 - N_EXPERTS)))
            x = _moe(x, mod, rt, moe_w_gate[layer // 2].astype(BF16),
                     moe_w_up[layer // 2].astype(BF16), moe_w_down[layer // 2].astype(BF16), lg, lb)
    return x
```

```python
import functools
import math

import jax
import jax.numpy as jnp
from jax import lax
from jax.experimental import pallas as pl
from jax.experimental.pallas import tpu as pltpu

F32 = jnp.float32
BF16 = jnp.bfloat16

D_MODEL = 1024
DEPTH = 4
HEAD_DIM = 64
N_Q_HEADS = 8
N_KV_HEADS = 2
Q_PER_KV = 4
ATTN_WIDTH = 512
KV_WIDTH = 128
SSM_WIDTH = 512
SSM_GROUP = 16
N_SSM_GROUPS = 32
SSM_STATE = 64
CMP_LEN = 32
CMP_STRIDE = 16
SLC_LEN = 64
N_SEL = 16
WINDOW = 512
Q_BLOCK = 128
D_FF = 2816
N_EXPERTS = 8
DEEPNORM_ALPHA = (2.0 * DEPTH) ** 0.25
LN_EPS = 1e-5
RMS_EPS = 1e-6

PROJ_PAD = 1920
SSM_CHUNK = 16
SLC_KEY_CHUNK = 512
NEG = -1e30
VMEM_LIMIT = 56 * 1024 * 1024


def _cparams(sem):
    return pltpu.CompilerParams(dimension_semantics=sem, vmem_limit_bytes=VMEM_LIMIT)


def _sigmoid(x):
    return 1.0 / (1.0 + jnp.exp(-x))


def _gelu(x):
    return 0.5 * x * (1.0 + jnp.tanh(math.sqrt(2.0 / math.pi) * (x + 0.044715 * (x * x * x))))


def _layer_norm(x, g, b):
    mu = jnp.mean(x, axis=-1, keepdims=True)
    xc = x - mu
    var = jnp.mean(xc * xc, axis=-1, keepdims=True)
    return xc * lax.rsqrt(var + LN_EPS) * g + b


def _rms_norm(x, g):
    return x * lax.rsqrt(jnp.mean(x * x, axis=-1, keepdims=True) + RMS_EPS) * g


def _dot(a, b):
    return jnp.dot(a, b, preferred_element_type=F32)


def _dot_nt(a, b, precision=None):
    return lax.dot_general(a, b, (((1,), (1,)), ((), ())), preferred_element_type=F32,
                           precision=precision)


def _ada_kernel(c_ref, w_ref, b_ref, o_ref):
    c = c_ref[...]
    ca = (c * _sigmoid(c)).astype(BF16)
    o_ref[0] = _dot(ca, w_ref[0].astype(BF16)) + b_ref[0]


def _ada_mod(c, ada_w, ada_b):
    depth, d, n = ada_w.shape
    b = c.shape[0]
    tn = 1536
    return pl.pallas_call(
        _ada_kernel,
        grid=(depth, n // tn),
        in_specs=[pl.BlockSpec((b, d), lambda l, j: (0, 0)),
                  pl.BlockSpec((1, d, tn), lambda l, j: (l, 0, j)),
                  pl.BlockSpec((1, 1, tn), lambda l, j: (l, 0, j))],
        out_specs=pl.BlockSpec((1, b, tn), lambda l, j: (l, 0, j)),
        out_shape=jax.ShapeDtypeStruct((depth, b, n), F32),
        compiler_params=_cparams(("parallel", "parallel")),
        name="ada_mod",
    )(c, ada_w, ada_b.reshape(depth, 1, n))


def _proj_kernel(x_ref, mod_ref, w_ref, q_ref, u_ref, kc_ref, vc_ref, ks_ref, vs_ref, kw_ref,
                 vw_ref, g_ref):
    m = mod_ref[0]
    h = x_ref[0] * (1.0 + m[1:2]) + m[0:1]
    r = _dot(h.astype(BF16), w_ref[...])
    q_ref[0] = (r[:, 0:512] * (HEAD_DIM ** -0.5)).astype(BF16)
    u_ref[0] = r[:, 512:1024]
    kc_ref[0] = r[:, 1024:1152]
    vc_ref[0] = r[:, 1152:1280]
    g_ref[0] = r[:, 1792:1920]
    tm = r.shape[0]
    lane = lax.broadcasted_iota(jnp.int32, (tm, 128), 1)
    key = pl.program_id(1) * tm + lax.broadcasted_iota(jnp.int32, (tm, 128), 0)
    lo_half = lane < HEAD_DIM
    block_onehot = jnp.where((key >> 6) == lane - HEAD_DIM, 1.0, 0.0)
    ones_col = jnp.where(lane == HEAD_DIM, 1.0, 0.0)
    for ref, col, const in ((ks_ref, 1280, block_onehot), (vs_ref, 1408, ones_col),
                            (kw_ref, 1536, 0.0), (vw_ref, 1664, ones_col)):
        slab = r[:, col:col + 128]
        ref[0, 0] = jnp.where(lo_half, slab, const).astype(BF16)
        ref[0, 1] = jnp.where(lo_half, pltpu.roll(slab, HEAD_DIM, 1), const).astype(BF16)


def _in_proj(x, mod, w_pad):
    b, l, d = x.shape
    tm = 512
    row = lambda n: pl.BlockSpec((1, tm, n), lambda i, j: (i, j, 0))
    shp = lambda n, dt: jax.ShapeDtypeStruct((b, l, n), dt)
    heads = pl.BlockSpec((1, N_KV_HEADS, tm, 128), lambda i, j: (i, 0, j, 0))
    hshp = jax.ShapeDtypeStruct((b, N_KV_HEADS, l, 128), BF16)
    return pl.pallas_call(
        _proj_kernel,
        grid=(b, l // tm),
        in_specs=[row(d),
                  pl.BlockSpec((1, 6, d), lambda i, j: (i, 0, 0)),
                  pl.BlockSpec((d, PROJ_PAD), lambda i, j: (0, 0))],
        out_specs=[row(512), row(512), row(128), row(128), heads, heads, heads, heads, row(128)],
        out_shape=[shp(512, BF16), shp(512, F32), shp(128, F32), shp(128, F32), hshp, hshp, hshp,
                   hshp, shp(128, F32)],
        compiler_params=_cparams(("parallel", "parallel")),
        name="in_proj",
    )(x, mod, w_pad)


def _compress_one(kr, pos_a, pos_b, w1a, w1b, w2):
    nch = kr.shape[0]
    pa = _dot((kr + pos_a).astype(BF16), w1a)
    pb = _dot((kr + pos_b).astype(BF16), w1b)
    hid = _gelu(pa + pltpu.roll(pb, nch - 1, 0))
    out = _dot(hid.astype(BF16), w2)
    rows = lax.broadcasted_iota(jnp.int32, out.shape, 0)
    return jnp.where(rows < nch - 1, out, 0.0)


def _compress_kernel(k_ref, v_ref, pak_ref, pbk_ref, w1ak_ref, w1bk_ref, w2k_ref,
                     pav_ref, pbv_ref, w1av_ref, w1bv_ref, w2v_ref, ko_ref, vo_ref):
    k = _compress_one(k_ref[0], pak_ref[...], pbk_ref[...], w1ak_ref[...], w1bk_ref[...], w2k_ref[...])
    v = _compress_one(v_ref[0], pav_ref[...], pbv_ref[...], w1av_ref[...], w1bv_ref[...], w2v_ref[...])
    lo_half = lax.broadcasted_iota(jnp.int32, k.shape, 1) < HEAD_DIM
    for ref, val in ((ko_ref, k), (vo_ref, v)):
        ref[0, 0] = jnp.where(lo_half, val, 0.0).astype(BF16)
        ref[0, 1] = jnp.where(lo_half, pltpu.roll(val, HEAD_DIM, 1), 0.0).astype(BF16)


def _compress_weights(pos, w1, w2):
    eye = jnp.eye(N_KV_HEADS, dtype=F32)
    half = CMP_STRIDE

    def big(w):
        return jnp.einsum('sdf,hg->shdgf', w, eye).reshape(half * KV_WIDTH, KV_WIDTH).astype(BF16)

    def posrow(p):
        return jnp.broadcast_to(p[:, None, :], (half, N_KV_HEADS, HEAD_DIM)).reshape(1, half * KV_WIDTH)

    w2b = jnp.einsum('fd,hg->hfgd', w2, eye).reshape(KV_WIDTH, KV_WIDTH).astype(BF16)
    return posrow(pos[:half]), posrow(pos[half:]), big(w1[:half]), big(w1[half:]), w2b


def _compress(kc, vc, wk, wv):
    b, l, _ = kc.shape
    nch = l // CMP_STRIDE
    kr = kc.reshape(b, nch, CMP_STRIDE * KV_WIDTH)
    vr = vc.reshape(b, nch, CMP_STRIDE * KV_WIDTH)
    full = lambda a: pl.BlockSpec(a.shape, lambda i: (0,) * a.ndim)
    blk = pl.BlockSpec((1, nch, CMP_STRIDE * KV_WIDTH), lambda i: (i, 0, 0))
    oblk = pl.BlockSpec((1, N_KV_HEADS, nch, KV_WIDTH), lambda i: (i, 0, 0, 0))
    return pl.pallas_call(
        _compress_kernel,
        grid=(b,),
        in_specs=[blk, blk] + [full(a) for a in wk] + [full(a) for a in wv],
        out_specs=[oblk, oblk],
        out_shape=[jax.ShapeDtypeStruct((b, N_KV_HEADS, nch, KV_WIDTH), BF16)] * 2,
        compiler_params=_cparams(("parallel",)),
        name="compress",
    )(kr, vr, *wk, *wv)


def _attn_kernel(q_ref, g_ref, kc_ref, vc_ref, ks_ref, vs_ref, kw_ref, vw_ref, o_ref, *, seq):
    nch = seq // CMP_STRIDE
    n_cmp = nch - 1
    n_slc = seq // SLC_LEN
    n_sel = min(N_SEL, n_slc)
    qb, gq, hd = Q_BLOCK, Q_PER_KV, HEAD_DIM
    nb = HEAD_DIM
    i = pl.program_id(1)
    qs = i * qb
    t_col = qs + lax.broadcasted_iota(jnp.int32, (qb, 1), 0)
    t_row = qs + lax.broadcasted_iota(jnp.int32, (1, qb), 1)
    lo_half = lax.broadcasted_iota(jnp.int32, (1, 128), 1) < hd

    gates = _sigmoid(g_ref[0])

    q_heads = []
    for m in range(N_Q_HEADS // 2):
        slab = q_ref[0, :, 128 * m:128 * (m + 1)]
        q_heads.append(slab)
        q_heads.append(pltpu.roll(slab.astype(F32), hd, 1).astype(BF16))

    n_idx = lax.broadcasted_iota(jnp.int32, (1, nch), 1)
    cmp_valid = ((n_idx * CMP_STRIDE + (CMP_LEN - 1)) <= t_col) & (n_idx < n_cmp)
    cmp_bias = jnp.where(cmp_valid, 0.0, NEG)
    cmp_keep = cmp_valid.astype(F32)
    oj = lax.broadcasted_iota(jnp.int32, (nb, nch), 0) * SLC_LEN
    on = lax.broadcasted_iota(jnp.int32, (nb, nch), 1) * CMP_STRIDE
    overlap_t = ((on < oj + SLC_LEN) & (on + CMP_LEN > oj)).astype(F32)

    j_col = lax.broadcasted_iota(jnp.int32, (nb, 1), 0)
    cur = t_row >> 6
    causal_t = (j_col * SLC_LEN) <= t_row
    forced_t = (j_col == 0) | (j_col == cur) | (j_col == cur - 1)
    sub = lax.broadcasted_iota(jnp.int32, (8, 1), 0)

    heads = range(N_Q_HEADS)
    kv_heads = range(N_KV_HEADS)
    kcb = [kc_ref[0, h] for h in kv_heads]
    vcb = [vc_ref[0, h] for h in kv_heads]
    s_c = [_dot_nt(q_heads[hq], kcb[hq // gq]) + cmp_bias for hq in heads]
    e_c = [jnp.exp(s_c[hq] - jnp.max(s_c[hq], axis=-1, keepdims=True)) * cmp_keep for hq in heads]
    den = [jnp.sum(e_c[hq], axis=-1, keepdims=True) for hq in heads]
    p_c = [e_c[hq] * (1.0 / jnp.where(den[hq] > 0, den[hq], 1.0)) for hq in heads]
    o_cmp = [_dot(p_c[hq].astype(BF16), vcb[hq // gq]) for hq in heads]
    p_sum = [p_c[gq * h] + p_c[gq * h + 1] + p_c[gq * h + 2] + p_c[gq * h + 3] for h in kv_heads]
    imp_t = [_dot_nt(overlap_t, p_sum[h], lax.Precision.HIGHEST) for h in kv_heads]

    score_t = [jnp.where(causal_t, jnp.where(forced_t, jnp.inf, imp_t[h]), -jnp.inf)
               for h in kv_heads]
    n_vb = nb // 8
    blocks = [[score_t[h][8 * v:8 * (v + 1)] for v in range(n_vb)] for h in kv_heads]
    ranks = [[jnp.zeros((8, qb), F32) for _ in range(n_vb)] for h in kv_heads]
    for b in range(n_slc):
        for h in kv_heads:
            row = jnp.broadcast_to(score_t[h][b:b + 1], (8, qb))
            for v in range(n_vb):
                if 8 * v > b:
                    beats = row >= blocks[h][v]
                elif 8 * v + 7 < b:
                    beats = row > blocks[h][v]
                else:
                    beats = (row > blocks[h][v]) | ((row == blocks[h][v]) & (sub + 8 * v > b))
                ranks[h][v] = ranks[h][v] + jnp.where(beats, 1.0, 0.0)
    q_sel = []
    for h in kv_heads:
        rank_t = jnp.concatenate(ranks[h], axis=0)
        sel_neg_t = jnp.where((rank_t < n_sel) & causal_t, 0.0, NEG)
        sel_neg = jnp.concatenate([sel_neg_t, sel_neg_t], axis=0).T.astype(BF16)
        for g in range(gq):
            q_sel.append(jnp.where(lo_half, q_heads[gq * h + g], sel_neg))

    n_tot = (qs + qb + SLC_KEY_CHUNK - 1) // SLC_KEY_CHUNK
    last_k0 = (n_tot - 1) * SLC_KEY_CHUNK
    kpos = last_k0 + lax.broadcasted_iota(jnp.int32, (1, SLC_KEY_CHUNK), 1)
    diag_bias = jnp.where(kpos <= t_col, 0.0, NEG)

    def slc_chunk(k0, carry, bias):
        heads = range(N_Q_HEADS)
        kblk = [ks_ref[0, h, pl.ds(k0, SLC_KEY_CHUNK), :] for h in range(N_KV_HEADS)]
        vblk = [vs_ref[0, h, pl.ds(k0, SLC_KEY_CHUNK), :] for h in range(N_KV_HEADS)]
        s = [_dot_nt(q_sel[hq], kblk[hq // gq]) for hq in heads]
        if bias is not None:
            s = [x + bias for x in s]
        m_new = [jnp.maximum(carry[hq][0], jnp.max(s[hq], axis=-1, keepdims=True)) for hq in heads]
        p = [jnp.exp(s[hq] - m_new[hq]).astype(BF16) for hq in heads]
        pv = [_dot(p[hq], vblk[hq // gq]) for hq in heads]
        return tuple((m_new[hq], jnp.exp(carry[hq][0] - m_new[hq]) * carry[hq][1] + pv[hq])
                     for hq in heads)

    init = tuple((jnp.full((qb, 1), NEG, F32), jnp.zeros((qb, 128), F32))
                 for _ in range(N_Q_HEADS))
    carry = lax.fori_loop(
        0, n_tot - 1,
        lambda c, cr: slc_chunk(pl.multiple_of(c * SLC_KEY_CHUNK, SLC_KEY_CHUNK), cr, None), init)
    carry = slc_chunk(pl.multiple_of(last_k0, SLC_KEY_CHUNK), carry, diag_bias)
    o_slc = [acc for _, acc in carry]

    win_start = pl.multiple_of(jnp.maximum(qs - WINDOW, 0), qb)
    wpos = win_start + lax.broadcasted_iota(jnp.int32, (1, WINDOW + qb), 1)
    dist = t_col - wpos
    win_bias = jnp.where((dist >= 0) & (dist < WINDOW), 0.0, NEG)

    kwb = [kw_ref[0, h, pl.ds(win_start, WINDOW + qb), :] for h in kv_heads]
    vwb = [vw_ref[0, h, pl.ds(win_start, WINDOW + qb), :] for h in kv_heads]
    s_w = [_dot_nt(q_heads[hq], kwb[hq // gq]) + win_bias for hq in heads]
    e_w = [jnp.exp(s_w[hq] - jnp.max(s_w[hq], axis=-1, keepdims=True)).astype(BF16)
           for hq in heads]
    o_win = [_dot(e_w[hq], vwb[hq // gq]) for hq in heads]

    res = []
    for hq in range(N_Q_HEADS):
        c0 = 3 * hq
        w_slc = gates[:, c0 + 1:c0 + 2] / o_slc[hq][:, hd:hd + 1]
        w_win = gates[:, c0 + 2:c0 + 3] / o_win[hq][:, hd:hd + 1]
        res.append(gates[:, c0:c0 + 1] * o_cmp[hq] + w_slc * o_slc[hq] + w_win * o_win[hq])
    for m in range(N_Q_HEADS // 2):
        o_ref[0, :, 128 * m:128 * (m + 1)] = jnp.where(
            lo_half, res[2 * m], pltpu.roll(res[2 * m + 1], hd, 1))


def _attention(q, g, k_cmp, v_cmp, ks, vs, kw, vw):
    b, l, _ = q.shape
    nch = l // CMP_STRIDE
    assert l // SLC_LEN <= HEAD_DIM and l >= WINDOW + Q_BLOCK and l % SLC_KEY_CHUNK == 0
    tile = lambda n: pl.BlockSpec((1, Q_BLOCK, n), lambda i, j: (i, j, 0))
    whole = lambda r: pl.BlockSpec((1, N_KV_HEADS, r, KV_WIDTH), lambda i, j: (i, 0, 0, 0))
    return pl.pallas_call(
        functools.partial(_attn_kernel, seq=l),
        grid=(b, l // Q_BLOCK),
        in_specs=[tile(ATTN_WIDTH), tile(128), whole(nch), whole(nch), whole(l), whole(l),
                  whole(l), whole(l)],
        out_specs=tile(ATTN_WIDTH),
        out_shape=jax.ShapeDtypeStruct((b, l, ATTN_WIDTH), F32),
        compiler_params=_cparams(("parallel", "arbitrary")),
        name="nsa_attention",
    )(q, g, k_cmp, v_cmp, ks, vs, kw, vw)


def _ssm_param_kernel(are_ref, aim_ref, ldt_ref, btr_ref, bti_ref, cr_ref, ci_ref,
                      t_ref, bcr_ref, bci_ref, ccr_ref, cci_ref, a16_ref):
    tc, hg, p = SSM_CHUNK, SSM_GROUP, SSM_STATE
    rows = tc * hg
    a_re = are_ref[0]
    a_im = aim_ref[0]
    dt = jnp.exp(ldt_ref[0])
    lam_re = a_re * dt
    lam_im = a_im * dt

    def apow(k):
        mag = jnp.exp(k * lam_re)
        return mag * jnp.cos(k * lam_im), mag * jnp.sin(k * lam_im)

    one = jnp.ones((1, 1), F32)
    ab_re, ab_im = apow(one)
    nr, ni = ab_re - 1.0, ab_im
    inv = 1.0 / (a_re * a_re + a_im * a_im)
    cf_re = (nr * a_re + ni * a_im) * inv
    cf_im = (ni * a_re - nr * a_im) * inv
    bt_re, bt_im = btr_ref[0], bti_ref[0]
    bb_re = cf_re * bt_re - cf_im * bt_im
    bb_im = cf_re * bt_im + cf_im * bt_re
    bb_re = jnp.concatenate([bb_re] * tc, axis=0)
    bb_im = jnp.concatenate([bb_im] * tc, axis=0)
    c_re = jnp.concatenate([cr_ref[0]] * tc, axis=0)
    c_im = jnp.concatenate([ci_ref[0]] * tc, axis=0)
    step = (lax.broadcasted_iota(jnp.int32, (rows, 1), 0) >> 4).astype(F32)

    def cmul(xr, xi, yr, yi):
        return xr * yr - xi * yi, xr * yi + xi * yr

    l_re, l_im = cmul(bb_re, bb_im, *apow(-step))
    r_re, r_im = cmul(c_re, c_im, *apow(step))
    hp = lax.Precision.HIGHEST
    tm = _dot_nt(l_re, r_re, hp) - _dot_nt(l_im, r_im, hp)
    s_row = lax.broadcasted_iota(jnp.int32, (rows, rows), 0) >> 4
    t_col = lax.broadcasted_iota(jnp.int32, (rows, rows), 1) >> 4
    t_ref[0] = jnp.where(t_col >= s_row, tm, 0.0).astype(BF16)
    e_re, e_im = cmul(bb_re, bb_im, *apow((tc - 1.0) - step))
    bcr_ref[0] = e_re.astype(BF16)
    bci_ref[0] = e_im.astype(BF16)
    f_re, f_im = cmul(c_re, c_im, *apow(step + 1.0))
    ccr_ref[0] = f_re.astype(BF16)
    cci_ref[0] = (-f_im).astype(BF16)
    p_re, p_im = apow(float(tc) * one)
    a16_ref[0] = jnp.concatenate([p_re, p_im], axis=0)


def _ssm_params(a_re, a_im, log_dt, b_re, b_im, c_re, c_im):
    g, p = a_re.shape
    hg, tc = SSM_GROUP, SSM_CHUNK
    rows = tc * hg
    vec = pl.BlockSpec((1, 1, p), lambda i: (i, 0, 0))
    mat = pl.BlockSpec((1, hg, p), lambda i: (i, 0, 0))
    out_rp = pl.BlockSpec((1, rows, p), lambda i: (i, 0, 0))
    return pl.pallas_call(
        _ssm_param_kernel,
        grid=(g,),
        in_specs=[vec, vec, pl.BlockSpec((1, 1, 1), lambda i: (i, 0, 0)), mat, mat, mat, mat],
        out_specs=[pl.BlockSpec((1, rows, rows), lambda i: (i, 0, 0)), out_rp, out_rp, out_rp, out_rp,
                   pl.BlockSpec((1, 2, p), lambda i: (i, 0, 0))],
        out_shape=[jax.ShapeDtypeStruct((g, rows, rows), BF16)] +
                  [jax.ShapeDtypeStruct((g, rows, p), BF16)] * 4 +
                  [jax.ShapeDtypeStruct((g, 2, p), F32)],
        compiler_params=_cparams(("parallel",)),
        name="ssm_params",
    )(a_re.reshape(g, 1, p), a_im.reshape(g, 1, p), log_dt.reshape(g, 1, 1),
      jnp.swapaxes(b_re, 1, 2), jnp.swapaxes(b_im, 1, 2), c_re, c_im)


SSM_TILE = 64
LANE_GROUPS = 8


def _ssm_kernel(u_ref, t_ref, bcr_ref, bci_ref, ccr_ref, cci_ref, are_ref, aim_ref, d_ref, y_ref,
                ug_scr, wre, wim, xsre, xsim, xre, xim):
    nc = u_ref.shape[1]
    n_groups, hg, tc = N_SSM_GROUPS, SSM_GROUP, SSM_CHUNK
    row_w = n_groups * hg
    piece = lax.broadcasted_iota(jnp.int32, (1, 128), 1) >> 4

    @pl.when(pl.program_id(1) == 0)
    def _():
        xre[...] = jnp.zeros_like(xre)
        xim[...] = jnp.zeros_like(xim)

    def pick(pieces):
        out = pieces[0]
        for k in range(1, LANE_GROUPS):
            out = jnp.where(piece == k, pieces[k], out)
        return out

    for g in range(n_groups):
        gb, g8 = divmod(g, LANE_GROUPS)
        cols = []
        for j in range(tc // LANE_GROUPS):
            pieces = []
            for s8 in range(LANE_GROUPS):
                s = LANE_GROUPS * j + s8
                src = u_ref[0, :, s * row_w + gb * 128:s * row_w + (gb + 1) * 128]
                shift = ((s8 - g8) * hg) % 128
                pieces.append(pltpu.roll(src, shift, 1) if shift else src)
            cols.append(pick(pieces))
        ug = jnp.concatenate(cols, axis=1)
        ug_scr[g] = ug
        ub = ug.astype(BF16)
        wre[g * nc:(g + 1) * nc, :] = _dot(ub, bcr_ref[g])
        wim[g * nc:(g + 1) * nc, :] = _dot(ub, bci_ref[g])

    a_r = are_ref[...]
    a_i = aim_ref[...]

    def step(c, carry):
        x_r, x_i = carry
        rows = pl.ds(c, n_groups, stride=nc)
        xsre[rows, :] = x_r
        xsim[rows, :] = x_i
        return a_r * x_r - a_i * x_i + wre[rows, :], a_r * x_i + a_i * x_r + wim[rows, :]

    x_r, x_i = lax.fori_loop(0, nc, step, (xre[...], xim[...]))
    xre[...] = x_r
    xim[...] = x_i

    for gb in range(n_groups // LANE_GROUPS):
        ys = []
        for g8 in range(LANE_GROUPS):
            g = gb * LANE_GROUPS + g8
            ug = ug_scr[g]
            rows = slice(g * nc, (g + 1) * nc)
            y = _dot(ug.astype(BF16), t_ref[g])
            y = y + _dot_nt(xsre[rows, :].astype(BF16), ccr_ref[g])
            y = y + _dot_nt(xsim[rows, :].astype(BF16), cci_ref[g])
            ys.append(y + d_ref[g] * ug)
        for t in range(tc):
            j, t8 = divmod(t, LANE_GROUPS)
            pieces = []
            for g8 in range(LANE_GROUPS):
                src = ys[g8][:, 128 * j:128 * (j + 1)]
                shift = ((g8 - t8) * hg) % 128
                pieces.append(pltpu.roll(src, shift, 1) if shift else src)
            y_ref[0, :, t * row_w + gb * 128:t * row_w + (gb + 1) * 128] = pick(pieces)


def _ssm(u, params, d_skip):
    b, l, _ = u.shape
    g, hg, tc, p = N_SSM_GROUPS, SSM_GROUP, SSM_CHUNK, SSM_STATE
    nch = l // tc
    nc = SSM_TILE
    w = tc * hg
    u3 = u.reshape(b, nch, tc * g * hg)
    d_t = jnp.broadcast_to(d_skip[:, None, :], (g, tc, hg)).reshape(g, 1, w)
    t_op, bcr, bci, ccr, cci, a16 = params
    full = lambda a: pl.BlockSpec(a.shape, lambda i, j: (0,) * a.ndim)
    consts = (t_op, bcr, bci, ccr, cci, a16[:, 0, :], a16[:, 1, :], d_t)
    tile = pl.BlockSpec((1, nc, tc * g * hg), lambda i, j: (i, j, 0))
    y3 = pl.pallas_call(
        _ssm_kernel,
        grid=(b, nch // nc),
        in_specs=[tile] + [full(a) for a in consts],
        out_specs=tile,
        out_shape=jax.ShapeDtypeStruct(u3.shape, F32),
        scratch_shapes=[pltpu.VMEM((g, nc, w), F32)] + [pltpu.VMEM((g * nc, p), F32)] * 4
                       + [pltpu.VMEM((g, p), F32)] * 2,
        compiler_params=_cparams(("parallel", "arbitrary")),
        name="ssm_scan",
    )(u3, *consts)
    return y3.reshape(b, l, g * hg)


def _mix_out_kernel(oa_ref, y_ref, x_ref, mod_ref, wglu_ref, woa_ref, wos_ref, na_ref, ns_ref,
                    lg_ref, lb_ref, o_ref):
    m = mod_ref[0]
    ya = _gelu(y_ref[0])
    o_ssm = ya * _sigmoid(_dot(ya.astype(BF16), wglu_ref[...]))
    ra = _rms_norm(oa_ref[0], na_ref[...])
    rs = _rms_norm(o_ssm, ns_ref[...])
    mix = _dot(ra.astype(BF16), woa_ref[...]) + _dot(rs.astype(BF16), wos_ref[...])
    o_ref[0] = _layer_norm(DEEPNORM_ALPHA * x_ref[0] + (1.0 + m[2:3]) * mix, lg_ref[...], lb_ref[...])


def _mix_out(o_attn, y, x, mod, w_glu, w_out_a, w_out_s, norm_attn, norm_ssm, ln_g, ln_b):
    b, l, d = x.shape
    tm = 512
    row = lambda n: pl.BlockSpec((1, tm, n), lambda i, j: (i, j, 0))
    full = lambda a: pl.BlockSpec(a.shape, lambda i, j: (0,) * a.ndim)
    consts = (w_glu, w_out_a, w_out_s, norm_attn, norm_ssm, ln_g, ln_b)
    return pl.pallas_call(
        _mix_out_kernel,
        grid=(b, l // tm),
        in_specs=[row(ATTN_WIDTH), row(SSM_WIDTH), row(d),
                  pl.BlockSpec((1, 6, d), lambda i, j: (i, 0, 0))] + [full(a) for a in consts],
        out_specs=row(d),
        out_shape=jax.ShapeDtypeStruct((b, l, d), F32),
        compiler_params=_cparams(("parallel", "parallel")),
        name="mix_out",
    )(o_attn, y, x, mod, *consts)


FF_CHUNK = 1408


def _ffn_kernel(x_ref, mod_ref, wg_ref, wu_ref, wd_ref, lg_ref, lb_ref, o_ref, h_scr, acc):
    j = pl.program_id(2)
    m = mod_ref[0]

    @pl.when(j == 0)
    def _():
        h_scr[...] = (x_ref[0] * (1.0 + m[4:5]) + m[3:4]).astype(BF16)
        acc[...] = jnp.zeros_like(acc)

    h = h_scr[...]
    gate = _dot(h, wg_ref[...])
    up = _dot(h, wu_ref[...])
    act = (gate * _sigmoid(gate) * up).astype(BF16)
    acc[...] += _dot(act, wd_ref[...])

    @pl.when(j == pl.num_programs(2) - 1)
    def _():
        o_ref[0] = _layer_norm(DEEPNORM_ALPHA * x_ref[0] + (1.0 + m[5:6]) * acc[...],
                               lg_ref[...], lb_ref[...])


def _ffn(x, mod, wg, wu, wd, ln_g, ln_b):
    b, l, d = x.shape
    tm = 512
    nf = D_FF // FF_CHUNK
    return pl.pallas_call(
        _ffn_kernel,
        grid=(b, l // tm, nf),
        in_specs=[pl.BlockSpec((1, tm, d), lambda i, r, j: (i, r, 0)),
                  pl.BlockSpec((1, 6, d), lambda i, r, j: (i, 0, 0)),
                  pl.BlockSpec((d, FF_CHUNK), lambda i, r, j: (0, j)),
                  pl.BlockSpec((d, FF_CHUNK), lambda i, r, j: (0, j)),
                  pl.BlockSpec((FF_CHUNK, d), lambda i, r, j: (j, 0)),
                  pl.BlockSpec((1, d), lambda i, r, j: (0, 0)),
                  pl.BlockSpec((1, d), lambda i, r, j: (0, 0))],
        out_specs=pl.BlockSpec((1, tm, d), lambda i, r, j: (i, r, 0)),
        out_shape=jax.ShapeDtypeStruct((b, l, d), F32),
        scratch_shapes=[pltpu.VMEM((tm, d), BF16), pltpu.VMEM((tm, d), F32)],
        compiler_params=_cparams(("parallel", "parallel", "arbitrary")),
        name="ffn",
    )(x, mod, wg, wu, wd, ln_g, ln_b)


MOE_TM = 512
META_E1, META_E2, META_W1, META_W2, META_P1, META_P2 = range(6)
REC_SUB = 8


def _store_records(ref, val):
    rows = val.shape[0]
    for s in range(REC_SUB):
        ref[pl.ds(s, rows, stride=REC_SUB), :] = val[:, 128 * s:128 * (s + 1)]


def _load_records(ref):
    rows = ref.shape[0] // REC_SUB
    return jnp.concatenate([ref[pl.ds(s, rows, stride=REC_SUB), :] for s in range(REC_SUB)], axis=1)


def _router_kernel(x_ref, mod_ref, rt_ref, h_ref, meta_ref, cnt_ref, run):
    i = pl.program_id(0)
    tm = x_ref.shape[0]

    @pl.when(i == 0)
    def _():
        run[...] = jnp.zeros_like(run)

    m = mod_ref[0]
    h = x_ref[...] * (1.0 + m[4:5]) + m[3:4]
    _store_records(h_ref, h)
    lane = lax.broadcasted_iota(jnp.int32, (tm, 128), 1)
    lane_f = lane.astype(F32)
    logits = jnp.dot(h, rt_ref[...], preferred_element_type=F32, precision=lax.Precision.HIGHEST)
    logits = jnp.where(lane < N_EXPERTS, logits, -jnp.inf)
    v1 = jnp.max(logits, axis=-1, keepdims=True)
    i1 = jnp.min(jnp.where(logits == v1, lane_f, 128.0), axis=-1, keepdims=True)
    rest = jnp.where(lane_f == i1, -jnp.inf, logits)
    v2 = jnp.max(rest, axis=-1, keepdims=True)
    i2 = jnp.min(jnp.where(rest == v2, lane_f, 128.0), axis=-1, keepdims=True)
    e2 = jnp.exp(v2 - v1)
    den = 1.0 + e2
    hit1 = lane_f == i1
    hit2 = lane_f == i2
    onehot = jnp.where(hit1 | hit2, 1.0, 0.0)
    earlier = (lax.broadcasted_iota(jnp.int32, (tm, tm), 0)
               > lax.broadcasted_iota(jnp.int32, (tm, tm), 1)).astype(BF16)
    slot = run[...] + _dot(earlier, onehot.astype(BF16))
    p1 = jnp.sum(jnp.where(hit1, slot, 0.0), axis=-1, keepdims=True)
    p2 = jnp.sum(jnp.where(hit2, slot, 0.0), axis=-1, keepdims=True)
    rec = jnp.zeros((tm, 128), F32)
    for col, val in ((META_E1, i1), (META_E2, i2), (META_W1, 1.0 / den), (META_W2, e2 / den),
                     (META_P1, p1), (META_P2, p2)):
        rec = jnp.where(lane == col, val, rec)
    meta_ref[...] = rec
    run[...] += jnp.sum(onehot, axis=0, keepdims=True)
    cnt_ref[...] = run[...]


def _router(xf, mod, router_pad, seq):
    t, d = xf.shape
    tm = MOE_TM
    return pl.pallas_call(
        _router_kernel,
        grid=(t // tm,),
        in_specs=[pl.BlockSpec((tm, d), lambda i: (i, 0)),
                  pl.BlockSpec((1, 6, d), lambda i: ((i * tm) // seq, 0, 0)),
                  pl.BlockSpec((d, 128), lambda i: (0, 0))],
        out_specs=[pl.BlockSpec((tm * REC_SUB, 128), lambda i: (i, 0)),
                   pl.BlockSpec((tm, 128), lambda i: (i, 0)),
                   pl.BlockSpec((1, 128), lambda i: (0, 0))],
        out_shape=[jax.ShapeDtypeStruct((t * REC_SUB, 128), F32),
                   jax.ShapeDtypeStruct((t, 128), F32), jax.ShapeDtypeStruct((1, 128), F32)],
        scratch_shapes=[pltpu.VMEM((1, 128), F32)],
        compiler_params=_cparams(("arbitrary",)),
        name="moe_router",
    )(xf, mod, router_pad)


def _dispatch_kernel(d1_ref, d2_ref, h_ref, hs_init, hs_hbm, sem):
    del hs_init
    tm = d1_ref.shape[-1]

    def record(ref, r):
        return ref.at[pl.ds(pl.multiple_of(r * REC_SUB, REC_SUB), REC_SUB)]

    def copies(t):
        src = record(h_ref, t)
        return (pltpu.make_async_copy(src, record(hs_hbm, d1_ref[0, 0, t]), sem),
                pltpu.make_async_copy(src, record(hs_hbm, d2_ref[0, 0, t]), sem))

    def start(t, carry):
        for cp in copies(t):
            cp.start()
        return carry

    def wait(t, carry):
        for cp in copies(t):
            cp.wait()
        return carry

    lax.fori_loop(0, tm, start, 0)
    lax.fori_loop(0, tm, wait, 0)


def _dispatch(h, dest1, dest2, n_rows):
    t = h.shape[0] // REC_SUB
    tm = MOE_TM
    idx = pl.BlockSpec((1, 1, tm), lambda i: (i, 0, 0), memory_space=pltpu.SMEM)
    anyspec = pl.BlockSpec(memory_space=pl.ANY)
    return pl.pallas_call(
        _dispatch_kernel,
        grid=(t // tm,),
        in_specs=[idx, idx, pl.BlockSpec((tm * REC_SUB, 128), lambda i: (i, 0)), anyspec],
        out_specs=anyspec,
        out_shape=jax.ShapeDtypeStruct((n_rows * REC_SUB, 128), F32),
        scratch_shapes=[pltpu.SemaphoreType.DMA(())],
        input_output_aliases={3: 0},
        compiler_params=_cparams(("arbitrary",)),
        name="moe_dispatch",
    )(dest1, dest2, h, jnp.zeros((n_rows * REC_SUB, 128), F32))


def _expert_kernel(te_ref, tu_ref, hs_ref, wg_ref, wu_ref, wd_ref, o_ref, h_scr, acc):
    del te_ref
    j = pl.program_id(0)
    f = pl.program_id(1)
    last = pl.num_programs(1) - 1
    used = tu_ref[j] == 1

    @pl.when(used)
    def _():
        @pl.when(f == 0)
        def _():
            h_scr[...] = _load_records(hs_ref).astype(BF16)
            acc[...] = jnp.zeros_like(acc)

        h = h_scr[...]
        gate = _dot(h, wg_ref[0])
        up = _dot(h, wu_ref[0])
        act = (gate * _sigmoid(gate) * up).astype(BF16)
        acc[...] += _dot(act, wd_ref[0])

        @pl.when(f == last)
        def _():
            _store_records(o_ref, acc[...])

    @pl.when(jnp.logical_not(used) & (f == last))
    def _():
        o_ref[...] = jnp.zeros_like(o_ref)


def _experts(hs, tile_expert, tile_used, wg, wu, wd):
    n_rows = hs.shape[0] // REC_SUB
    d = D_MODEL
    tm = MOE_TM
    nf = D_FF // FF_CHUNK
    rec = pl.BlockSpec((tm * REC_SUB, 128), lambda j, f, te, tu: (j, 0))
    grid_spec = pltpu.PrefetchScalarGridSpec(
        num_scalar_prefetch=2,
        grid=(n_rows // tm, nf),
        in_specs=[rec,
                  pl.BlockSpec((1, d, FF_CHUNK), lambda j, f, te, tu: (te[j], 0, f)),
                  pl.BlockSpec((1, d, FF_CHUNK), lambda j, f, te, tu: (te[j], 0, f)),
                  pl.BlockSpec((1, FF_CHUNK, d), lambda j, f, te, tu: (te[j], f, 0))],
        out_specs=rec,
        scratch_shapes=[pltpu.VMEM((tm, d), BF16), pltpu.VMEM((tm, d), F32)])
    return pl.pallas_call(
        _expert_kernel,
        grid_spec=grid_spec,
        out_shape=jax.ShapeDtypeStruct((n_rows * REC_SUB, 128), F32),
        compiler_params=_cparams(("arbitrary", "arbitrary")),
        name="moe_experts",
    )(tile_expert, tile_used, hs, wg, wu, wd)


def _combine_kernel(d1_ref, d2_ref, x_ref, mod_ref, meta_ref, lg_ref, lb_ref, ys_hbm, o_ref,
                    buf, sem):
    tm = x_ref.shape[0]

    def record(ref, r):
        return ref.at[pl.ds(pl.multiple_of(r * REC_SUB, REC_SUB), REC_SUB)]

    def copies(t):
        return (pltpu.make_async_copy(record(ys_hbm, d1_ref[0, 0, t]), record(buf.at[0], t), sem),
                pltpu.make_async_copy(record(ys_hbm, d2_ref[0, 0, t]), record(buf.at[1], t), sem))

    def start(t, carry):
        for cp in copies(t):
            cp.start()
        return carry

    def wait(t, carry):
        for cp in copies(t):
            cp.wait()
        return carry

    lax.fori_loop(0, tm, start, 0)
    lax.fori_loop(0, tm, wait, 0)
    m = mod_ref[0]
    rec = meta_ref[...]
    f = (rec[:, META_W1:META_W1 + 1] * _load_records(buf.at[0])
         + rec[:, META_W2:META_W2 + 1] * _load_records(buf.at[1]))
    o_ref[...] = _layer_norm(DEEPNORM_ALPHA * x_ref[...] + (1.0 + m[5:6]) * f,
                             lg_ref[...], lb_ref[...])


def _combine(xf, mod, meta, dest1, dest2, ys, ln_g, ln_b, seq):
    t, d = xf.shape
    tm = MOE_TM
    idx = pl.BlockSpec((1, 1, tm), lambda i: (i, 0, 0), memory_space=pltpu.SMEM)
    return pl.pallas_call(
        _combine_kernel,
        grid=(t // tm,),
        in_specs=[idx, idx,
                  pl.BlockSpec((tm, d), lambda i: (i, 0)),
                  pl.BlockSpec((1, 6, d), lambda i: ((i * tm) // seq, 0, 0)),
                  pl.BlockSpec((tm, 128), lambda i: (i, 0)),
                  pl.BlockSpec((1, d), lambda i: (0, 0)),
                  pl.BlockSpec((1, d), lambda i: (0, 0)),
                  pl.BlockSpec(memory_space=pl.ANY)],
        out_specs=pl.BlockSpec((tm, d), lambda i: (i, 0)),
        out_shape=jax.ShapeDtypeStruct((t, d), F32),
        scratch_shapes=[pltpu.VMEM((2, tm * REC_SUB, 128), F32), pltpu.SemaphoreType.DMA(())],
        compiler_params=_cparams(("arbitrary",)),
        name="moe_combine",
    )(dest1, dest2, xf, mod, meta, ln_g, ln_b, ys)


def _moe(x, mod, router_pad, wg, wu, wd, ln_g, ln_b):
    b, l, d = x.shape
    t = b * l
    tm = MOE_TM
    n_tiles = 2 * t // tm + N_EXPERTS
    xf = x.reshape(t, d)
    h, meta, cnt = _router(xf, mod, router_pad, l)
    counts = cnt[0, :N_EXPERTS].astype(jnp.int32)
    tiles_e = (counts + tm - 1) // tm
    tile_end = jnp.cumsum(tiles_e)
    base = (tile_end - tiles_e) * tm
    e1 = meta[:, META_E1].astype(jnp.int32)
    e2 = meta[:, META_E2].astype(jnp.int32)
    dest1 = (base[e1] + meta[:, META_P1].astype(jnp.int32)).reshape(t // tm, 1, tm)
    dest2 = (base[e2] + meta[:, META_P2].astype(jnp.int32)).reshape(t // tm, 1, tm)
    tile_id = jnp.arange(n_tiles, dtype=jnp.int32)
    tile_expert = jnp.minimum(jnp.sum(tile_id[:, None] >= tile_end[None, :], axis=1),
                              N_EXPERTS - 1).astype(jnp.int32)
    tile_used = (tile_id < tile_end[-1]).astype(jnp.int32)
    hs = _dispatch(h, dest1, dest2, n_tiles * tm)
    ys = _experts(hs, tile_expert, tile_used, wg, wu, wd)
    out = _combine(xf, mod, meta, dest1, dest2, ys, ln_g, ln_b, l)
    return out.reshape(b, l, d)


def _reorder_w_in(w):
    q = w[:, :ATTN_WIDTH]
    kv = w[:, ATTN_WIDTH:ATTN_WIDTH + 6 * KV_WIDTH]
    g0 = ATTN_WIDTH + 6 * KV_WIDTH
    gates = w[:, g0:g0 + 24]
    u = w[:, g0 + 24:]
    pad = jnp.zeros((w.shape[0], PROJ_PAD - w.shape[1]), w.dtype)
    return jnp.concatenate([q, u, kv, gates, pad], axis=1).astype(BF16)


def kernel(x, c, w_in, cmp_pos_k, cmp_pos_v, cmp_w1_k, cmp_w2_k, cmp_w1_v, cmp_w2_v, ssm_a_re,
           ssm_a_im, ssm_log_dt, ssm_b_re, ssm_b_im, ssm_c_re, ssm_c_im, ssm_d, ssm_w_glu, norm_attn,
           norm_ssm, w_out, ada_w, ada_b, ln_g, ln_b, ffn_w_gate, ffn_w_up, ffn_w_down, moe_router,
           moe_w_gate, moe_w_up, moe_w_down):
    b, l, d = x.shape
    mod_all = _ada_mod(c, ada_w, ada_b).reshape(DEPTH, b, 6, d)
    for layer in range(DEPTH):
        mod = mod_all[layer]
        q, u, kc, vc, ks, vs, kw, vw, g = _in_proj(x, mod, _reorder_w_in(w_in[layer]))
        k_cmp, v_cmp = _compress(
            kc, vc,
            _compress_weights(cmp_pos_k[layer], cmp_w1_k[layer], cmp_w2_k[layer]),
            _compress_weights(cmp_pos_v[layer], cmp_w1_v[layer], cmp_w2_v[layer]))
        o_attn = _attention(q, g, k_cmp, v_cmp, ks, vs, kw, vw)
        params = _ssm_params(ssm_a_re[layer], ssm_a_im[layer], ssm_log_dt[layer], ssm_b_re[layer],
                             ssm_b_im[layer], ssm_c_re[layer], ssm_c_im[layer])
        y = _ssm(u, params, ssm_d[layer])
        wo = w_out[layer].astype(BF16)
        x = _mix_out(o_attn, y, x, mod, ssm_w_glu[layer].astype(BF16), wo[:ATTN_WIDTH],
                     wo[ATTN_WIDTH:], norm_attn[layer][None], norm_ssm[layer][None],
                     ln_g[layer, 0][None], ln_b[layer, 0][None])
        lg, lb = ln_g[layer, 1][None], ln_b[layer, 1][None]
        if layer % 2 == 0:
            x = _ffn(x, mod, ffn_w_gate[layer // 2].astype(BF16), ffn_w_up[layer // 2].astype(BF16),
                     ffn_w_down[layer // 2].astype(BF16), lg, lb)
        else:
            rt = jnp.pad(moe_router[layer // 2], ((0, 0), (0, 128 - N_EXPERTS)))
            x = _moe(x, mod, rt, moe_w_gate[layer // 2].astype(BF16),
                     moe_w_up[layer // 2].astype(BF16), moe_w_down[layer // 2].astype(BF16), lg, lb)
    return x
```

```python
import functools
import math

import jax
import jax.numpy as jnp
from jax import lax
from jax.experimental import pallas as pl
from jax.experimental.pallas import tpu as pltpu

F32 = jnp.float32
BF16 = jnp.bfloat16

D_MODEL = 1024
DEPTH = 4
HEAD_DIM = 64
N_Q_HEADS = 8
N_KV_HEADS = 2
Q_PER_KV = 4
ATTN_WIDTH = 512
KV_WIDTH = 128
SSM_WIDTH = 512
SSM_GROUP = 16
N_SSM_GROUPS = 32
SSM_STATE = 64
CMP_LEN = 32
CMP_STRIDE = 16
SLC_LEN = 64
N_SEL = 16
WINDOW = 512
Q_BLOCK = 128
D_FF = 2816
N_EXPERTS = 8
DEEPNORM_ALPHA = (2.0 * DEPTH) ** 0.25
LN_EPS = 1e-5
RMS_EPS = 1e-6

PROJ_PAD = 1920
SSM_CHUNK = 16
SLC_KEY_CHUNK = 512
NEG = -1e30
VMEM_LIMIT = 56 * 1024 * 1024


def _cparams(sem):
    return pltpu.CompilerParams(dimension_semantics=sem, vmem_limit_bytes=VMEM_LIMIT)


def _sigmoid(x):
    return 1.0 / (1.0 + jnp.exp(-x))


def _gelu(x):
    return 0.5 * x * (1.0 + jnp.tanh(math.sqrt(2.0 / math.pi) * (x + 0.044715 * (x * x * x))))


def _layer_norm(x, g, b):
    mu = jnp.mean(x, axis=-1, keepdims=True)
    xc = x - mu
    var = jnp.mean(xc * xc, axis=-1, keepdims=True)
    return xc * lax.rsqrt(var + LN_EPS) * g + b


def _rms_norm(x, g):
    return x * lax.rsqrt(jnp.mean(x * x, axis=-1, keepdims=True) + RMS_EPS) * g


def _dot(a, b):
    return jnp.dot(a, b, preferred_element_type=F32)


def _dot_nt(a, b, precision=None):
    return lax.dot_general(a, b, (((1,), (1,)), ((), ())), preferred_element_type=F32,
                           precision=precision)


def _ada_kernel(c_ref, w_ref, b_ref, o_ref):
    c = c_ref[...]
    ca = (c * _sigmoid(c)).astype(BF16)
    o_ref[0] = _dot(ca, w_ref[0].astype(BF16)) + b_ref[0]


def _ada_mod(c, ada_w, ada_b):
    depth, d, n = ada_w.shape
    b = c.shape[0]
    tn = 1536
    return pl.pallas_call(
        _ada_kernel,
        grid=(depth, n // tn),
        in_specs=[pl.BlockSpec((b, d), lambda l, j: (0, 0)),
                  pl.BlockSpec((1, d, tn), lambda l, j: (l, 0, j)),
                  pl.BlockSpec((1, 1, tn), lambda l, j: (l, 0, j))],
        out_specs=pl.BlockSpec((1, b, tn), lambda l, j: (l, 0, j)),
        out_shape=jax.ShapeDtypeStruct((depth, b, n), F32),
        compiler_params=_cparams(("parallel", "parallel")),
        name="ada_mod",
    )(c, ada_w, ada_b.reshape(depth, 1, n))


def _proj_kernel(x_ref, mod_ref, w_ref, q_ref, u_ref, kc_ref, vc_ref, ks_ref, vs_ref, kw_ref,
                 vw_ref, g_ref):
    m = mod_ref[0]
    h = x_ref[0] * (1.0 + m[1:2]) + m[0:1]
    r = _dot(h.astype(BF16), w_ref[...])
    q_ref[0] = (r[:, 0:512] * (HEAD_DIM ** -0.5 * math.log2(math.e))).astype(BF16)
    u_ref[0] = r[:, 512:1024]
    kc_ref[0] = r[:, 1024:1152]
    vc_ref[0] = r[:, 1152:1280]
    g_ref[0] = r[:, 1792:1920]
    tm = r.shape[0]
    lane = lax.broadcasted_iota(jnp.int32, (tm, 128), 1)
    key = pl.program_id(1) * tm + lax.broadcasted_iota(jnp.int32, (tm, 128), 0)
    lo_half = lane < HEAD_DIM
    block_onehot = jnp.where((key >> 6) == lane - HEAD_DIM, 1.0, 0.0)
    ones_col = jnp.where(lane == HEAD_DIM, 1.0, 0.0)
    for ref, col, const in ((ks_ref, 1280, block_onehot), (vs_ref, 1408, ones_col),
                            (kw_ref, 1536, 0.0), (vw_ref, 1664, ones_col)):
        slab = r[:, col:col + 128]
        ref[0, 0] = jnp.where(lo_half, slab, const).astype(BF16)
        ref[0, 1] = jnp.where(lo_half, pltpu.roll(slab, HEAD_DIM, 1), const).astype(BF16)


def _in_proj(x, mod, w_pad):
    b, l, d = x.shape
    tm = 512
    row = lambda n: pl.BlockSpec((1, tm, n), lambda i, j: (i, j, 0))
    shp = lambda n, dt: jax.ShapeDtypeStruct((b, l, n), dt)
    heads = pl.BlockSpec((1, N_KV_HEADS, tm, 128), lambda i, j: (i, 0, j, 0))
    hshp = jax.ShapeDtypeStruct((b, N_KV_HEADS, l, 128), BF16)
    return pl.pallas_call(
        _proj_kernel,
        grid=(b, l // tm),
        in_specs=[row(d),
                  pl.BlockSpec((1, 6, d), lambda i, j: (i, 0, 0)),
                  pl.BlockSpec((d, PROJ_PAD), lambda i, j: (0, 0))],
        out_specs=[row(512), row(512), row(128), row(128), heads, heads, heads, heads, row(128)],
        out_shape=[shp(512, BF16), shp(512, F32), shp(128, F32), shp(128, F32), hshp, hshp, hshp,
                   hshp, shp(128, F32)],
        compiler_params=_cparams(("parallel", "parallel")),
        name="in_proj",
    )(x, mod, w_pad)


def _compress_one(kr, pos_a, pos_b, w1a, w1b, w2):
    nch = kr.shape[0]
    pa = _dot((kr + pos_a).astype(BF16), w1a)
    pb = _dot((kr + pos_b).astype(BF16), w1b)
    hid = _gelu(pa + pltpu.roll(pb, nch - 1, 0))
    out = _dot(hid.astype(BF16), w2)
    rows = lax.broadcasted_iota(jnp.int32, out.shape, 0)
    return jnp.where(rows < nch - 1, out, 0.0)


def _compress_kernel(k_ref, v_ref, pak_ref, pbk_ref, w1ak_ref, w1bk_ref, w2k_ref,
                     pav_ref, pbv_ref, w1av_ref, w1bv_ref, w2v_ref, ko_ref, vo_ref):
    k = _compress_one(k_ref[0], pak_ref[...], pbk_ref[...], w1ak_ref[...], w1bk_ref[...], w2k_ref[...])
    v = _compress_one(v_ref[0], pav_ref[...], pbv_ref[...], w1av_ref[...], w1bv_ref[...], w2v_ref[...])
    lo_half = lax.broadcasted_iota(jnp.int32, k.shape, 1) < HEAD_DIM
    for ref, val in ((ko_ref, k), (vo_ref, v)):
        ref[0, 0] = jnp.where(lo_half, val, 0.0).astype(BF16)
        ref[0, 1] = jnp.where(lo_half, pltpu.roll(val, HEAD_DIM, 1), 0.0).astype(BF16)


def _compress_weights(pos, w1, w2):
    eye = jnp.eye(N_KV_HEADS, dtype=F32)
    half = CMP_STRIDE

    def big(w):
        return jnp.einsum('sdf,hg->shdgf', w, eye).reshape(half * KV_WIDTH, KV_WIDTH).astype(BF16)

    def posrow(p):
        return jnp.broadcast_to(p[:, None, :], (half, N_KV_HEADS, HEAD_DIM)).reshape(1, half * KV_WIDTH)

    w2b = jnp.einsum('fd,hg->hfgd', w2, eye).reshape(KV_WIDTH, KV_WIDTH).astype(BF16)
    return posrow(pos[:half]), posrow(pos[half:]), big(w1[:half]), big(w1[half:]), w2b


def _compress(kc, vc, wk, wv):
    b, l, _ = kc.shape
    nch = l // CMP_STRIDE
    kr = kc.reshape(b, nch, CMP_STRIDE * KV_WIDTH)
    vr = vc.reshape(b, nch, CMP_STRIDE * KV_WIDTH)
    full = lambda a: pl.BlockSpec(a.shape, lambda i: (0,) * a.ndim)
    blk = pl.BlockSpec((1, nch, CMP_STRIDE * KV_WIDTH), lambda i: (i, 0, 0))
    oblk = pl.BlockSpec((1, N_KV_HEADS, nch, KV_WIDTH), lambda i: (i, 0, 0, 0))
    return pl.pallas_call(
        _compress_kernel,
        grid=(b,),
        in_specs=[blk, blk] + [full(a) for a in wk] + [full(a) for a in wv],
        out_specs=[oblk, oblk],
        out_shape=[jax.ShapeDtypeStruct((b, N_KV_HEADS, nch, KV_WIDTH), BF16)] * 2,
        compiler_params=_cparams(("parallel",)),
        name="compress",
    )(kr, vr, *wk, *wv)


def _attn_kernel(q_ref, g_ref, kc_ref, vc_ref, ks_ref, vs_ref, kw_ref, vw_ref, o_ref, *, seq):
    nch = seq // CMP_STRIDE
    n_cmp = nch - 1
    n_slc = seq // SLC_LEN
    n_sel = min(N_SEL, n_slc)
    qb, gq, hd = Q_BLOCK, Q_PER_KV, HEAD_DIM
    nb = HEAD_DIM
    i = pl.program_id(1)
    qs = i * qb
    t_col = qs + lax.broadcasted_iota(jnp.int32, (qb, 1), 0)
    t_row = qs + lax.broadcasted_iota(jnp.int32, (1, qb), 1)
    lo_half = lax.broadcasted_iota(jnp.int32, (1, 128), 1) < hd

    gates = _sigmoid(g_ref[0])

    q_heads = []
    for m in range(N_Q_HEADS // 2):
        slab = q_ref[0, :, 128 * m:128 * (m + 1)]
        q_heads.append(slab)
        q_heads.append(pltpu.roll(slab.astype(F32), hd, 1).astype(BF16))

    n_idx = lax.broadcasted_iota(jnp.int32, (1, nch), 1)
    cmp_valid = ((n_idx * CMP_STRIDE + (CMP_LEN - 1)) <= t_col) & (n_idx < n_cmp)
    cmp_bias = jnp.where(cmp_valid, 0.0, NEG)
    cmp_keep = cmp_valid.astype(F32)
    oj = lax.broadcasted_iota(jnp.int32, (nb, nch), 0) * SLC_LEN
    on = lax.broadcasted_iota(jnp.int32, (nb, nch), 1) * CMP_STRIDE
    overlap_t = ((on < oj + SLC_LEN) & (on + CMP_LEN > oj)).astype(F32)

    j_col = lax.broadcasted_iota(jnp.int32, (nb, 1), 0)
    cur = t_row >> 6
    causal_t = (j_col * SLC_LEN) <= t_row
    forced_t = (j_col == 0) | (j_col == cur) | (j_col == cur - 1)
    sub = lax.broadcasted_iota(jnp.int32, (8, 1), 0)

    heads = range(N_Q_HEADS)
    kv_heads = range(N_KV_HEADS)
    kcb = [kc_ref[0, h] for h in kv_heads]
    vcb = [vc_ref[0, h] for h in kv_heads]
    s_c = [_dot_nt(q_heads[hq], kcb[hq // gq]) + cmp_bias for hq in heads]
    e_c = [jnp.exp2(s_c[hq] - jnp.max(s_c[hq], axis=-1, keepdims=True)) * cmp_keep for hq in heads]
    den = [jnp.sum(e_c[hq], axis=-1, keepdims=True) for hq in heads]
    p_c = [e_c[hq] * (1.0 / jnp.where(den[hq] > 0, den[hq], 1.0)) for hq in heads]
    o_cmp = [_dot(p_c[hq].astype(BF16), vcb[hq // gq]) for hq in heads]
    p_sum = [p_c[gq * h] + p_c[gq * h + 1] + p_c[gq * h + 2] + p_c[gq * h + 3] for h in kv_heads]
    imp_t = [_dot_nt(overlap_t, p_sum[h], lax.Precision.HIGHEST) for h in kv_heads]

    score_t = [jnp.where(causal_t, jnp.where(forced_t, jnp.inf, imp_t[h]), -jnp.inf)
               for h in kv_heads]
    n_vb = nb // 8
    blocks = [[score_t[h][8 * v:8 * (v + 1)] for v in range(n_vb)] for h in kv_heads]
    ranks = [[jnp.zeros((8, qb), F32) for _ in range(n_vb)] for h in kv_heads]
    for b in range(n_slc):
        for h in kv_heads:
            row = jnp.broadcast_to(score_t[h][b:b + 1], (8, qb))
            for v in range(n_vb):
                if 8 * v > b:
                    beats = row >= blocks[h][v]
                elif 8 * v + 7 < b:
                    beats = row > blocks[h][v]
                else:
                    beats = (row > blocks[h][v]) | ((row == blocks[h][v]) & (sub + 8 * v > b))
                ranks[h][v] = ranks[h][v] + jnp.where(beats, 1.0, 0.0)
    q_sel = []
    for h in kv_heads:
        rank_t = jnp.concatenate(ranks[h], axis=0)
        sel_neg_t = jnp.where((rank_t < n_sel) & causal_t, 0.0, NEG)
        sel_neg = jnp.concatenate([sel_neg_t, sel_neg_t], axis=0).T.astype(BF16)
        for g in range(gq):
            q_sel.append(jnp.where(lo_half, q_heads[gq * h + g], sel_neg))

    n_tot = (qs + qb + SLC_KEY_CHUNK - 1) // SLC_KEY_CHUNK
    last_k0 = (n_tot - 1) * SLC_KEY_CHUNK
    kpos = last_k0 + lax.broadcasted_iota(jnp.int32, (1, SLC_KEY_CHUNK), 1)
    diag_bias = jnp.where(kpos <= t_col, 0.0, NEG)

    def slc_chunk(k0, carry, bias):
        heads = range(N_Q_HEADS)
        kblk = [ks_ref[0, h, pl.ds(k0, SLC_KEY_CHUNK), :] for h in range(N_KV_HEADS)]
        vblk = [vs_ref[0, h, pl.ds(k0, SLC_KEY_CHUNK), :] for h in range(N_KV_HEADS)]
        s = [_dot_nt(q_sel[hq], kblk[hq // gq]) for hq in heads]
        if bias is not None:
            s = [x + bias for x in s]
        m_new = [jnp.maximum(carry[hq][0], jnp.max(s[hq], axis=-1, keepdims=True)) for hq in heads]
        p = [jnp.exp2((s[hq] - m_new[hq]).astype(BF16)) for hq in heads]
        pv = [_dot(p[hq], vblk[hq // gq]) for hq in heads]
        return tuple((m_new[hq], jnp.exp2(carry[hq][0] - m_new[hq]) * carry[hq][1] + pv[hq])
                     for hq in heads)

    init = tuple((jnp.full((qb, 1), NEG, F32), jnp.zeros((qb, 128), F32))
                 for _ in range(N_Q_HEADS))
    carry = lax.fori_loop(
        0, n_tot - 1,
        lambda c, cr: slc_chunk(pl.multiple_of(c * SLC_KEY_CHUNK, SLC_KEY_CHUNK), cr, None), init)
    carry = slc_chunk(pl.multiple_of(last_k0, SLC_KEY_CHUNK), carry, diag_bias)
    o_slc = [acc for _, acc in carry]

    win_start = pl.multiple_of(jnp.maximum(qs - WINDOW, 0), qb)
    wpos = win_start + lax.broadcasted_iota(jnp.int32, (1, WINDOW + qb), 1)
    dist = t_col - wpos
    win_bias = jnp.where((dist >= 0) & (dist < WINDOW), 0.0, NEG)

    kwb = [kw_ref[0, h, pl.ds(win_start, WINDOW + qb), :] for h in kv_heads]
    vwb = [vw_ref[0, h, pl.ds(win_start, WINDOW + qb), :] for h in kv_heads]
    s_w = [_dot_nt(q_heads[hq], kwb[hq // gq]) + win_bias for hq in heads]
    e_w = [jnp.exp2((s_w[hq] - jnp.max(s_w[hq], axis=-1, keepdims=True)).astype(BF16))
           for hq in heads]
    o_win = [_dot(e_w[hq], vwb[hq // gq]) for hq in heads]

    res = []
    for hq in range(N_Q_HEADS):
        c0 = 3 * hq
        w_slc = gates[:, c0 + 1:c0 + 2] / o_slc[hq][:, hd:hd + 1]
        w_win = gates[:, c0 + 2:c0 + 3] / o_win[hq][:, hd:hd + 1]
        res.append(gates[:, c0:c0 + 1] * o_cmp[hq] + w_slc * o_slc[hq] + w_win * o_win[hq])
    for m in range(N_Q_HEADS // 2):
        o_ref[0, :, 128 * m:128 * (m + 1)] = jnp.where(
            lo_half, res[2 * m], pltpu.roll(res[2 * m + 1], hd, 1))


def _attention(q, g, k_cmp, v_cmp, ks, vs, kw, vw):
    b, l, _ = q.shape
    nch = l // CMP_STRIDE
    assert l // SLC_LEN <= HEAD_DIM and l >= WINDOW + Q_BLOCK and l % SLC_KEY_CHUNK == 0
    tile = lambda n: pl.BlockSpec((1, Q_BLOCK, n), lambda i, j: (i, j, 0))
    whole = lambda r: pl.BlockSpec((1, N_KV_HEADS, r, KV_WIDTH), lambda i, j: (i, 0, 0, 0))
    return pl.pallas_call(
        functools.partial(_attn_kernel, seq=l),
        grid=(b, l // Q_BLOCK),
        in_specs=[tile(ATTN_WIDTH), tile(128), whole(nch), whole(nch), whole(l), whole(l),
                  whole(l), whole(l)],
        out_specs=tile(ATTN_WIDTH),
        out_shape=jax.ShapeDtypeStruct((b, l, ATTN_WIDTH), F32),
        compiler_params=_cparams(("parallel", "arbitrary")),
        name="nsa_attention",
    )(q, g, k_cmp, v_cmp, ks, vs, kw, vw)


def _ssm_param_kernel(are_ref, aim_ref, ldt_ref, btr_ref, bti_ref, cr_ref, ci_ref,
                      t_ref, bcr_ref, bci_ref, ccr_ref, cci_ref, a16_ref):
    tc, hg, p = SSM_CHUNK, SSM_GROUP, SSM_STATE
    rows = tc * hg
    a_re = are_ref[0]
    a_im = aim_ref[0]
    dt = jnp.exp(ldt_ref[0])
    lam_re = a_re * dt
    lam_im = a_im * dt

    def apow(k):
        mag = jnp.exp(k * lam_re)
        return mag * jnp.cos(k * lam_im), mag * jnp.sin(k * lam_im)

    one = jnp.ones((1, 1), F32)
    ab_re, ab_im = apow(one)
    nr, ni = ab_re - 1.0, ab_im
    inv = 1.0 / (a_re * a_re + a_im * a_im)
    cf_re = (nr * a_re + ni * a_im) * inv
    cf_im = (ni * a_re - nr * a_im) * inv
    bt_re, bt_im = btr_ref[0], bti_ref[0]
    bb_re = cf_re * bt_re - cf_im * bt_im
    bb_im = cf_re * bt_im + cf_im * bt_re
    bb_re = jnp.concatenate([bb_re] * tc, axis=0)
    bb_im = jnp.concatenate([bb_im] * tc, axis=0)
    c_re = jnp.concatenate([cr_ref[0]] * tc, axis=0)
    c_im = jnp.concatenate([ci_ref[0]] * tc, axis=0)
    step = (lax.broadcasted_iota(jnp.int32, (rows, 1), 0) >> 4).astype(F32)

    def cmul(xr, xi, yr, yi):
        return xr * yr - xi * yi, xr * yi + xi * yr

    l_re, l_im = cmul(bb_re, bb_im, *apow(-step))
    r_re, r_im = cmul(c_re, c_im, *apow(step))
    hp = lax.Precision.HIGHEST
    tm = _dot_nt(l_re, r_re, hp) - _dot_nt(l_im, r_im, hp)
    s_row = lax.broadcasted_iota(jnp.int32, (rows, rows), 0) >> 4
    t_col = lax.broadcasted_iota(jnp.int32, (rows, rows), 1) >> 4
    t_ref[0] = jnp.where(t_col >= s_row, tm, 0.0).astype(BF16)
    e_re, e_im = cmul(bb_re, bb_im, *apow((tc - 1.0) - step))
    bcr_ref[0] = e_re.astype(BF16)
    bci_ref[0] = e_im.astype(BF16)
    f_re, f_im = cmul(c_re, c_im, *apow(step + 1.0))
    ccr_ref[0] = f_re.astype(BF16)
    cci_ref[0] = (-f_im).astype(BF16)
    p_re, p_im = apow(float(tc) * one)
    a16_ref[0] = jnp.concatenate([p_re, p_im], axis=0)


def _ssm_params(a_re, a_im, log_dt, b_re, b_im, c_re, c_im):
    g, p = a_re.shape
    hg, tc = SSM_GROUP, SSM_CHUNK
    rows = tc * hg
    vec = pl.BlockSpec((1, 1, p), lambda i: (i, 0, 0))
    mat = pl.BlockSpec((1, hg, p), lambda i: (i, 0, 0))
    out_rp = pl.BlockSpec((1, rows, p), lambda i: (i, 0, 0))
    return pl.pallas_call(
        _ssm_param_kernel,
        grid=(g,),
        in_specs=[vec, vec, pl.BlockSpec((1, 1, 1), lambda i: (i, 0, 0)), mat, mat, mat, mat],
        out_specs=[pl.BlockSpec((1, rows, rows), lambda i: (i, 0, 0)), out_rp, out_rp, out_rp, out_rp,
                   pl.BlockSpec((1, 2, p), lambda i: (i, 0, 0))],
        out_shape=[jax.ShapeDtypeStruct((g, rows, rows), BF16)] +
                  [jax.ShapeDtypeStruct((g, rows, p), BF16)] * 4 +
                  [jax.ShapeDtypeStruct((g, 2, p), F32)],
        compiler_params=_cparams(("parallel",)),
        name="ssm_params",
    )(a_re.reshape(g, 1, p), a_im.reshape(g, 1, p), log_dt.reshape(g, 1, 1),
      jnp.swapaxes(b_re, 1, 2), jnp.swapaxes(b_im, 1, 2), c_re, c_im)


SSM_TILE = 64
LANE_GROUPS = 8


def _ssm_kernel(u_ref, t_ref, bcr_ref, bci_ref, ccr_ref, cci_ref, are_ref, aim_ref, d_ref, y_ref,
                ug_scr, wre, wim, xsre, xsim, xre, xim):
    nc = u_ref.shape[1]
    n_groups, hg, tc = N_SSM_GROUPS, SSM_GROUP, SSM_CHUNK
    row_w = n_groups * hg
    piece = lax.broadcasted_iota(jnp.int32, (1, 128), 1) >> 4

    @pl.when(pl.program_id(1) == 0)
    def _():
        xre[...] = jnp.zeros_like(xre)
        xim[...] = jnp.zeros_like(xim)

    def transpose_pieces(xs):
        xs = list(xs)
        for k in (4, 2, 1):
            upper = (piece & k) != 0
            for i in range(LANE_GROUPS):
                if i & k:
                    continue
                a, b = xs[i], xs[i + k]
                xs[i] = jnp.where(upper, pltpu.roll(b, k * hg, 1), a)
                xs[i + k] = jnp.where(upper, b, pltpu.roll(a, 128 - k * hg, 1))
        return xs

    for gb in range(n_groups // LANE_GROUPS):
        cols = []
        for j in range(tc // LANE_GROUPS):
            cols.append(transpose_pieces(
                [u_ref[0, :, (LANE_GROUPS * j + s8) * row_w + gb * 128:
                       (LANE_GROUPS * j + s8) * row_w + (gb + 1) * 128]
                 for s8 in range(LANE_GROUPS)]))
        for g8 in range(LANE_GROUPS):
            g = gb * LANE_GROUPS + g8
            ug = jnp.concatenate([c[g8] for c in cols], axis=1)
            ug_scr[g] = ug
            ub = ug.astype(BF16)
            wre[pl.ds(g, nc, stride=n_groups), :] = _dot(ub, bcr_ref[g])
            wim[pl.ds(g, nc, stride=n_groups), :] = _dot(ub, bci_ref[g])

    a_r = are_ref[...]
    a_i = aim_ref[...]

    def step(c, carry):
        x_r, x_i = carry
        rows = pl.ds(pl.multiple_of(c * n_groups, n_groups), n_groups)
        xsre[rows, :] = x_r
        xsim[rows, :] = x_i
        return a_r * x_r - a_i * x_i + wre[rows, :], a_r * x_i + a_i * x_r + wim[rows, :]

    x_r, x_i = lax.fori_loop(0, nc, step, (xre[...], xim[...]))
    xre[...] = x_r
    xim[...] = x_i

    for gb in range(n_groups // LANE_GROUPS):
        ys = []
        for g8 in range(LANE_GROUPS):
            g = gb * LANE_GROUPS + g8
            ug = ug_scr[g]
            rows = pl.ds(g, nc, stride=n_groups)
            y = _dot(ug.astype(BF16), t_ref[g])
            y = y + _dot_nt(xsre[rows, :].astype(BF16), ccr_ref[g])
            y = y + _dot_nt(xsim[rows, :].astype(BF16), cci_ref[g])
            ys.append(y + d_ref[g] * ug)
        for j in range(tc // LANE_GROUPS):
            outs = transpose_pieces([y[:, 128 * j:128 * (j + 1)] for y in ys])
            for t8 in range(LANE_GROUPS):
                t = LANE_GROUPS * j + t8
                y_ref[0, :, t * row_w + gb * 128:t * row_w + (gb + 1) * 128] = outs[t8]


def _ssm(u, params, d_skip):
    b, l, _ = u.shape
    g, hg, tc, p = N_SSM_GROUPS, SSM_GROUP, SSM_CHUNK, SSM_STATE
    nch = l // tc
    nc = SSM_TILE
    w = tc * hg
    u3 = u.reshape(b, nch, tc * g * hg)
    d_t = jnp.broadcast_to(d_skip[:, None, :], (g, tc, hg)).reshape(g, 1, w)
    t_op, bcr, bci, ccr, cci, a16 = params
    full = lambda a: pl.BlockSpec(a.shape, lambda i, j: (0,) * a.ndim)
    consts = (t_op, bcr, bci, ccr, cci, a16[:, 0, :], a16[:, 1, :], d_t)
    tile = pl.BlockSpec((1, nc, tc * g * hg), lambda i, j: (i, j, 0))
    y3 = pl.pallas_call(
        _ssm_kernel,
        grid=(b, nch // nc),
        in_specs=[tile] + [full(a) for a in consts],
        out_specs=tile,
        out_shape=jax.ShapeDtypeStruct(u3.shape, F32),
        scratch_shapes=[pltpu.VMEM((g, nc, w), F32)] + [pltpu.VMEM((g * nc, p), F32)] * 4
                       + [pltpu.VMEM((g, p), F32)] * 2,
        compiler_params=_cparams(("parallel", "arbitrary")),
        name="ssm_scan",
    )(u3, *consts)
    return y3.reshape(b, l, g * hg)


def _mix_out_kernel(oa_ref, y_ref, x_ref, mod_ref, wglu_ref, woa_ref, wos_ref, na_ref, ns_ref,
                    lg_ref, lb_ref, o_ref):
    m = mod_ref[0]
    ya = _gelu(y_ref[0])
    o_ssm = ya * _sigmoid(_dot(ya.astype(BF16), wglu_ref[...]))
    ra = _rms_norm(oa_ref[0], na_ref[...])
    rs = _rms_norm(o_ssm, ns_ref[...])
    mix = _dot(ra.astype(BF16), woa_ref[...]) + _dot(rs.astype(BF16), wos_ref[...])
    o_ref[0] = _layer_norm(DEEPNORM_ALPHA * x_ref[0] + (1.0 + m[2:3]) * mix, lg_ref[...], lb_ref[...])


def _mix_out(o_attn, y, x, mod, w_glu, w_out_a, w_out_s, norm_attn, norm_ssm, ln_g, ln_b):
    b, l, d = x.shape
    tm = 512
    row = lambda n: pl.BlockSpec((1, tm, n), lambda i, j: (i, j, 0))
    full = lambda a: pl.BlockSpec(a.shape, lambda i, j: (0,) * a.ndim)
    consts = (w_glu, w_out_a, w_out_s, norm_attn, norm_ssm, ln_g, ln_b)
    return pl.pallas_call(
        _mix_out_kernel,
        grid=(b, l // tm),
        in_specs=[row(ATTN_WIDTH), row(SSM_WIDTH), row(d),
                  pl.BlockSpec((1, 6, d), lambda i, j: (i, 0, 0))] + [full(a) for a in consts],
        out_specs=row(d),
        out_shape=jax.ShapeDtypeStruct((b, l, d), F32),
        compiler_params=_cparams(("parallel", "parallel")),
        name="mix_out",
    )(o_attn, y, x, mod, *consts)


FF_CHUNK = 1408


def _ffn_kernel(x_ref, mod_ref, wg_ref, wu_ref, wd_ref, lg_ref, lb_ref, o_ref, h_scr, acc):
    j = pl.program_id(2)
    m = mod_ref[0]

    @pl.when(j == 0)
    def _():
        h_scr[...] = (x_ref[0] * (1.0 + m[4:5]) + m[3:4]).astype(BF16)
        acc[...] = jnp.zeros_like(acc)

    h = h_scr[...]
    gate = _dot(h, wg_ref[...])
    up = _dot(h, wu_ref[...])
    act = (gate * _sigmoid(gate) * up).astype(BF16)
    acc[...] += _dot(act, wd_ref[...])

    @pl.when(j == pl.num_programs(2) - 1)
    def _():
        o_ref[0] = _layer_norm(DEEPNORM_ALPHA * x_ref[0] + (1.0 + m[5:6]) * acc[...],
                               lg_ref[...], lb_ref[...])


def _ffn(x, mod, wg, wu, wd, ln_g, ln_b):
    b, l, d = x.shape
    tm = 512
    nf = D_FF // FF_CHUNK
    return pl.pallas_call(
        _ffn_kernel,
        grid=(b, l // tm, nf),
        in_specs=[pl.BlockSpec((1, tm, d), lambda i, r, j: (i, r, 0)),
                  pl.BlockSpec((1, 6, d), lambda i, r, j: (i, 0, 0)),
                  pl.BlockSpec((d, FF_CHUNK), lambda i, r, j: (0, j)),
                  pl.BlockSpec((d, FF_CHUNK), lambda i, r, j: (0, j)),
                  pl.BlockSpec((FF_CHUNK, d), lambda i, r, j: (j, 0)),
                  pl.BlockSpec((1, d), lambda i, r, j: (0, 0)),
                  pl.BlockSpec((1, d), lambda i, r, j: (0, 0))],
        out_specs=pl.BlockSpec((1, tm, d), lambda i, r, j: (i, r, 0)),
        out_shape=jax.ShapeDtypeStruct((b, l, d), F32),
        scratch_shapes=[pltpu.VMEM((tm, d), BF16), pltpu.VMEM((tm, d), F32)],
        compiler_params=_cparams(("parallel", "parallel", "arbitrary")),
        name="ffn",
    )(x, mod, wg, wu, wd, ln_g, ln_b)


MOE_TM = 512
META_E1, META_E2, META_W1, META_W2, META_P1, META_P2 = range(6)
REC_SUB = 8


def _store_records(ref, val):
    rows = val.shape[0]
    for s in range(REC_SUB):
        ref[pl.ds(s, rows, stride=REC_SUB), :] = val[:, 128 * s:128 * (s + 1)]


def _load_records(ref):
    rows = ref.shape[0] // REC_SUB
    return jnp.concatenate([ref[pl.ds(s, rows, stride=REC_SUB), :] for s in range(REC_SUB)], axis=1)


def _router_kernel(x_ref, mod_ref, rt_ref, h_ref, meta_ref, cnt_ref, run):
    i = pl.program_id(0)
    tm = x_ref.shape[0]

    @pl.when(i == 0)
    def _():
        run[...] = jnp.zeros_like(run)

    m = mod_ref[0]
    h = x_ref[...] * (1.0 + m[4:5]) + m[3:4]
    _store_records(h_ref, h)
    lane = lax.broadcasted_iota(jnp.int32, (tm, 128), 1)
    lane_f = lane.astype(F32)
    logits = jnp.dot(h, rt_ref[...], preferred_element_type=F32, precision=lax.Precision.HIGHEST)
    logits = jnp.where(lane < N_EXPERTS, logits, -jnp.inf)
    v1 = jnp.max(logits, axis=-1, keepdims=True)
    i1 = jnp.min(jnp.where(logits == v1, lane_f, 128.0), axis=-1, keepdims=True)
    rest = jnp.where(lane_f == i1, -jnp.inf, logits)
    v2 = jnp.max(rest, axis=-1, keepdims=True)
    i2 = jnp.min(jnp.where(rest == v2, lane_f, 128.0), axis=-1, keepdims=True)
    e2 = jnp.exp(v2 - v1)
    den = 1.0 + e2
    hit1 = lane_f == i1
    hit2 = lane_f == i2
    onehot = jnp.where(hit1 | hit2, 1.0, 0.0)
    earlier = (lax.broadcasted_iota(jnp.int32, (tm, tm), 0)
               > lax.broadcasted_iota(jnp.int32, (tm, tm), 1)).astype(BF16)
    slot = run[...] + _dot(earlier, onehot.astype(BF16))
    p1 = jnp.sum(jnp.where(hit1, slot, 0.0), axis=-1, keepdims=True)
    p2 = jnp.sum(jnp.where(hit2, slot, 0.0), axis=-1, keepdims=True)
    rec = jnp.zeros((tm, 128), F32)
    for col, val in ((META_E1, i1), (META_E2, i2), (META_W1, 1.0 / den), (META_W2, e2 / den),
                     (META_P1, p1), (META_P2, p2)):
        rec = jnp.where(lane == col, val, rec)
    meta_ref[...] = rec
    run[...] += jnp.sum(onehot, axis=0, keepdims=True)
    cnt_ref[...] = run[...]


def _router(xf, mod, router_pad, seq):
    t, d = xf.shape
    tm = MOE_TM
    return pl.pallas_call(
        _router_kernel,
        grid=(t // tm,),
        in_specs=[pl.BlockSpec((tm, d), lambda i: (i, 0)),
                  pl.BlockSpec((1, 6, d), lambda i: ((i * tm) // seq, 0, 0)),
                  pl.BlockSpec((d, 128), lambda i: (0, 0))],
        out_specs=[pl.BlockSpec((tm * REC_SUB, 128), lambda i: (i, 0)),
                   pl.BlockSpec((tm, 128), lambda i: (i, 0)),
                   pl.BlockSpec((1, 128), lambda i: (0, 0))],
        out_shape=[jax.ShapeDtypeStruct((t * REC_SUB, 128), F32),
                   jax.ShapeDtypeStruct((t, 128), F32), jax.ShapeDtypeStruct((1, 128), F32)],
        scratch_shapes=[pltpu.VMEM((1, 128), F32)],
        compiler_params=_cparams(("arbitrary",)),
        name="moe_router",
    )(xf, mod, router_pad)


def _dispatch_kernel(d1_ref, d2_ref, h_ref, hs_init, hs_hbm, sem):
    del hs_init
    tm = d1_ref.shape[-1]

    def record(ref, r):
        return ref.at[pl.ds(pl.multiple_of(r * REC_SUB, REC_SUB), REC_SUB)]

    def copies(t):
        src = record(h_ref, t)
        return (pltpu.make_async_copy(src, record(hs_hbm, d1_ref[0, 0, t]), sem),
                pltpu.make_async_copy(src, record(hs_hbm, d2_ref[0, 0, t]), sem))

    def start(t, carry):
        for cp in copies(t):
            cp.start()
        return carry

    def wait(t, carry):
        for cp in copies(t):
            cp.wait()
        return carry

    lax.fori_loop(0, tm, start, 0)
    lax.fori_loop(0, tm, wait, 0)


def _dispatch(h, dest1, dest2, n_rows):
    t = h.shape[0] // REC_SUB
    tm = MOE_TM
    idx = pl.BlockSpec((1, 1, tm), lambda i: (i, 0, 0), memory_space=pltpu.SMEM)
    anyspec = pl.BlockSpec(memory_space=pl.ANY)
    return pl.pallas_call(
        _dispatch_kernel,
        grid=(t // tm,),
        in_specs=[idx, idx, pl.BlockSpec((tm * REC_SUB, 128), lambda i: (i, 0)), anyspec],
        out_specs=anyspec,
        out_shape=jax.ShapeDtypeStruct((n_rows * REC_SUB, 128), F32),
        scratch_shapes=[pltpu.SemaphoreType.DMA(())],
        input_output_aliases={3: 0},
        compiler_params=_cparams(("arbitrary",)),
        name="moe_dispatch",
    )(dest1, dest2, h, jnp.zeros((n_rows * REC_SUB, 128), F32))


def _expert_kernel(te_ref, tu_ref, hs_ref, wg_ref, wu_ref, wd_ref, o_ref, h_scr, acc):
    del te_ref
    j = pl.program_id(0)
    f = pl.program_id(1)
    last = pl.num_programs(1) - 1
    used = tu_ref[j] == 1

    @pl.when(used)
    def _():
        @pl.when(f == 0)
        def _():
            h_scr[...] = _load_records(hs_ref).astype(BF16)
            acc[...] = jnp.zeros_like(acc)

        h = h_scr[...]
        gate = _dot(h, wg_ref[0])
        up = _dot(h, wu_ref[0])
        act = (gate * _sigmoid(gate) * up).astype(BF16)
        acc[...] += _dot(act, wd_ref[0])

        @pl.when(f == last)
        def _():
            _store_records(o_ref, acc[...])

    @pl.when(jnp.logical_not(used) & (f == last))
    def _():
        o_ref[...] = jnp.zeros_like(o_ref)


def _experts(hs, tile_expert, tile_used, wg, wu, wd):
    n_rows = hs.shape[0] // REC_SUB
    d = D_MODEL
    tm = MOE_TM
    nf = D_FF // FF_CHUNK
    rec = pl.BlockSpec((tm * REC_SUB, 128), lambda j, f, te, tu: (j, 0))
    grid_spec = pltpu.PrefetchScalarGridSpec(
        num_scalar_prefetch=2,
        grid=(n_rows // tm, nf),
        in_specs=[rec,
                  pl.BlockSpec((1, d, FF_CHUNK), lambda j, f, te, tu: (te[j], 0, f)),
                  pl.BlockSpec((1, d, FF_CHUNK), lambda j, f, te, tu: (te[j], 0, f)),
                  pl.BlockSpec((1, FF_CHUNK, d), lambda j, f, te, tu: (te[j], f, 0))],
        out_specs=rec,
        scratch_shapes=[pltpu.VMEM((tm, d), BF16), pltpu.VMEM((tm, d), F32)])
    return pl.pallas_call(
        _expert_kernel,
        grid_spec=grid_spec,
        out_shape=jax.ShapeDtypeStruct((n_rows * REC_SUB, 128), F32),
        compiler_params=_cparams(("arbitrary", "arbitrary")),
        name="moe_experts",
    )(tile_expert, tile_used, hs, wg, wu, wd)


def _combine_kernel(d1_ref, d2_ref, x_ref, mod_ref, meta_ref, lg_ref, lb_ref, ys_hbm, o_ref,
                    buf, sem):
    tm = x_ref.shape[0]

    def record(ref, r):
        return ref.at[pl.ds(pl.multiple_of(r * REC_SUB, REC_SUB), REC_SUB)]

    def copies(t):
        return (pltpu.make_async_copy(record(ys_hbm, d1_ref[0, 0, t]), record(buf.at[0], t), sem),
                pltpu.make_async_copy(record(ys_hbm, d2_ref[0, 0, t]), record(buf.at[1], t), sem))

    def start(t, carry):
        for cp in copies(t):
            cp.start()
        return carry

    def wait(t, carry):
        for cp in copies(t):
            cp.wait()
        return carry

    lax.fori_loop(0, tm, start, 0)
    lax.fori_loop(0, tm, wait, 0)
    m = mod_ref[0]
    rec = meta_ref[...]
    f = (rec[:, META_W1:META_W1 + 1] * _load_records(buf.at[0])
         + rec[:, META_W2:META_W2 + 1] * _load_records(buf.at[1]))
    o_ref[...] = _layer_norm(DEEPNORM_ALPHA * x_ref[...] + (1.0 + m[5:6]) * f,
                             lg_ref[...], lb_ref[...])


def _combine(xf, mod, meta, dest1, dest2, ys, ln_g, ln_b, seq):
    t, d = xf.shape
    tm = MOE_TM
    idx = pl.BlockSpec((1, 1, tm), lambda i: (i, 0, 0), memory_space=pltpu.SMEM)
    return pl.pallas_call(
        _combine_kernel,
        grid=(t // tm,),
        in_specs=[idx, idx,
                  pl.BlockSpec((tm, d), lambda i: (i, 0)),
                  pl.BlockSpec((1, 6, d), lambda i: ((i * tm) // seq, 0, 0)),
                  pl.BlockSpec((tm, 128), lambda i: (i, 0)),
                  pl.BlockSpec((1, d), lambda i: (0, 0)),
                  pl.BlockSpec((1, d), lambda i: (0, 0)),
                  pl.BlockSpec(memory_space=pl.ANY)],
        out_specs=pl.BlockSpec((tm, d), lambda i: (i, 0)),
        out_shape=jax.ShapeDtypeStruct((t, d), F32),
        scratch_shapes=[pltpu.VMEM((2, tm * REC_SUB, 128), F32), pltpu.SemaphoreType.DMA(())],
        compiler_params=_cparams(("arbitrary",)),
        name="moe_combine",
    )(dest1, dest2, xf, mod, meta, ln_g, ln_b, ys)


def _moe(x, mod, router_pad, wg, wu, wd, ln_g, ln_b):
    b, l, d = x.shape
    t = b * l
    tm = MOE_TM
    n_tiles = 2 * t // tm + N_EXPERTS
    xf = x.reshape(t, d)
    h, meta, cnt = _router(xf, mod, router_pad, l)
    counts = cnt[0, :N_EXPERTS].astype(jnp.int32)
    tiles_e = (counts + tm - 1) // tm
    tile_end = jnp.cumsum(tiles_e)
    base = (tile_end - tiles_e) * tm
    e1 = meta[:, META_E1].astype(jnp.int32)
    e2 = meta[:, META_E2].astype(jnp.int32)
    dest1 = (base[e1] + meta[:, META_P1].astype(jnp.int32)).reshape(t // tm, 1, tm)
    dest2 = (base[e2] + meta[:, META_P2].astype(jnp.int32)).reshape(t // tm, 1, tm)
    tile_id = jnp.arange(n_tiles, dtype=jnp.int32)
    tile_expert = jnp.minimum(jnp.sum(tile_id[:, None] >= tile_end[None, :], axis=1),
                              N_EXPERTS - 1).astype(jnp.int32)
    tile_used = (tile_id < tile_end[-1]).astype(jnp.int32)
    hs = _dispatch(h, dest1, dest2, n_tiles * tm)
    ys = _experts(hs, tile_expert, tile_used, wg, wu, wd)
    out = _combine(xf, mod, meta, dest1, dest2, ys, ln_g, ln_b, l)
    return out.reshape(b, l, d)


def _reorder_w_in(w):
    q = w[:, :ATTN_WIDTH]
    kv = w[:, ATTN_WIDTH:ATTN_WIDTH + 6 * KV_WIDTH]
    g0 = ATTN_WIDTH + 6 * KV_WIDTH
    gates = w[:, g0:g0 + 24]
    u = w[:, g0 + 24:]
    pad = jnp.zeros((w.shape[0], PROJ_PAD - w.shape[1]), w.dtype)
    return jnp.concatenate([q, u, kv, gates, pad], axis=1).astype(BF16)


def kernel(x, c, w_in, cmp_pos_k, cmp_pos_v, cmp_w1_k, cmp_w2_k, cmp_w1_v, cmp_w2_v, ssm_a_re,
           ssm_a_im, ssm_log_dt, ssm_b_re, ssm_b_im, ssm_c_re, ssm_c_im, ssm_d, ssm_w_glu, norm_attn,
           norm_ssm, w_out, ada_w, ada_b, ln_g, ln_b, ffn_w_gate, ffn_w_up, ffn_w_down, moe_router,
           moe_w_gate, moe_w_up, moe_w_down):
    b, l, d = x.shape
    mod_all = _ada_mod(c, ada_w, ada_b).reshape(DEPTH, b, 6, d)
    for layer in range(DEPTH):
        mod = mod_all[layer]
        q, u, kc, vc, ks, vs, kw, vw, g = _in_proj(x, mod, _reorder_w_in(w_in[layer]))
        k_cmp, v_cmp = _compress(
            kc, vc,
            _compress_weights(cmp_pos_k[layer], cmp_w1_k[layer], cmp_w2_k[layer]),
            _compress_weights(cmp_pos_v[layer], cmp_w1_v[layer], cmp_w2_v[layer]))
        o_attn = _attention(q, g, k_cmp, v_cmp, ks, vs, kw, vw)
        params = _ssm_params(ssm_a_re[layer], ssm_a_im[layer], ssm_log_dt[layer], ssm_b_re[layer],
                             ssm_b_im[layer], ssm_c_re[layer], ssm_c_im[layer])
        y = _ssm(u, params, ssm_d[layer])
        wo = w_out[layer].astype(BF16)
        x = _mix_out(o_attn, y, x, mod, ssm_w_glu[layer].astype(BF16), wo[:ATTN_WIDTH],
                     wo[ATTN_WIDTH:], norm_attn[layer][None], norm_ssm[layer][None],
                     ln_g[layer, 0][None], ln_b[layer, 0][None])
        lg, lb = ln_g[layer, 1][None], ln_b[layer, 1][None]
        if layer % 2 == 0:
            x = _ffn(x, mod, ffn_w_gate[layer // 2].astype(BF16), ffn_w_up[layer // 2].astype(BF16),
                     ffn_w_down[layer // 2].astype(BF16), lg, lb)
        else:
            rt = jnp.pad(moe_router[layer // 2], ((0, 0), (0, 128 - N_EXPERTS)))
            x = _moe(x, mod, rt, moe_w_gate[layer // 2].astype(BF16),
                     moe_w_up[layer // 2].astype(BF16), moe_w_down[layer // 2].astype(BF16), lg, lb)
    return x
```
